```python
import math
import jax, jax.numpy as jnp
from jax import lax
import numpy as np

D_MODEL = 2048
BATCH = 16
SEQ = 2048
DEPTH = 2

MEM_LEN = 256
D_MIX = D_MODEL
D_ATTN = D_MIX // 2
DQ = 64
DV = 2 * DQ
N_HEADS_A = D_ATTN // DV
D_CONV = D_MIX - D_ATTN
CONV_WIDTH = 31
N_BUCKETS = 32
MAX_DISTANCE = 128
N_HEADS_C = 4
DH_C = 128
D_CROSS = N_HEADS_C * DH_C
N_GROUPS = 8
EXPERTS_PER_GROUP = 8
N_EXPERTS = N_GROUPS * EXPERTS_PER_GROUP
TOP_K = 2
D_EXPERT = D_MODEL // 4
Q_BLOCK = 128
MOE_BLOCK = 256
EPS = 1e-6
NEG_INF = -1e30
D_QK = N_HEADS_A * 2 * DQ
D_IN = 2 * D_QK + N_HEADS_A * DV + 2 * D_CONV

kernel_name = 'hybrid_diffattn_conformer_hmoe'


def rmsnorm(x, g):
    xf = x.astype(jnp.float32)
    y = xf * lax.rsqrt(jnp.mean(xf * xf, axis=-1, keepdims=True) + EPS)
    return (y * g.astype(jnp.float32)).astype(x.dtype)


def layernorm(x, g, b):
    xf = x.astype(jnp.float32)
    mu = jnp.mean(xf, axis=-1, keepdims=True)
    var = jnp.mean(jnp.square(xf - mu), axis=-1, keepdims=True)
    y = (xf - mu) * lax.rsqrt(var + EPS)
    return (y * g.astype(jnp.float32) + b.astype(jnp.float32)).astype(x.dtype)


def t5_causal_bucket(dist):
    max_exact = N_BUCKETS // 2
    d_f = jnp.maximum(dist, 1).astype(jnp.float32)
    large = max_exact + (jnp.log(d_f / max_exact) / math.log(MAX_DISTANCE / max_exact)
                         * (N_BUCKETS - max_exact)).astype(jnp.int32)
    large = jnp.minimum(large, N_BUCKETS - 1)
    return jnp.where(dist < max_exact, dist, large)


def diff_attention(q, k, v, lam, lam_init, g_q, g_k, g_subln, rel_bias_dist):
    B, S = q.shape[:2]
    q = rmsnorm(q, g_q) * (DQ ** -0.5)
    k = rmsnorm(k, g_k)
    outs = []
    for i in range(S // Q_BLOCK):
        q0, qe = i * Q_BLOCK, (i + 1) * Q_BLOCK
        s = jnp.einsum('bqhmd,bkhmd->bhmqk', q[:, q0:qe], k[:, :qe]).astype(jnp.float32)
        dist = jnp.arange(q0, qe)[:, None] - jnp.arange(qe)[None, :]
        bias = rel_bias_dist[:, jnp.maximum(dist, 0)].astype(jnp.float32)
        s = jnp.where(dist >= 0, s + bias[None, :, None], NEG_INF)
        p = jax.nn.softmax(s, axis=-1)
        a = (p[:, :, 0] - lam * p[:, :, 1]).astype(v.dtype)
        outs.append(jnp.einsum('bhqk,bkhd->bqhd', a, v[:, :qe]))
    o = jnp.concatenate(outs, axis=1)
    o = rmsnorm(o, g_subln) * (1.0 - lam_init)
    return o.reshape(B, S, N_HEADS_A * DV)


def conformer_conv(c, conv_w, conv_b, ln_g, ln_b):
    a, gate = jnp.split(c, 2, axis=-1)
    u = a * jax.nn.sigmoid(gate)
    u = lax.conv_general_dilated(u, conv_w[:, None, :], window_strides=(1,),
                                 padding=[(CONV_WIDTH - 1, 0)],
                                 dimension_numbers=('NWC', 'WIO', 'NWC'),
                                 feature_group_count=D_CONV) + conv_b
    u = layernorm(u, ln_g, ln_b)
    return jax.nn.silu(u)


def cross_attention(h, mem_n, wq, wkv, g_qc, g_kc, wo):
    B, S, _ = h.shape
    M = mem_n.shape[1]
    q = rmsnorm((h @ wq).reshape(B, S, N_HEADS_C, DH_C), g_qc) * (DH_C ** -0.5)
    kv = (mem_n @ wkv).reshape(B, M, 2, N_HEADS_C, DH_C)
    k = rmsnorm(kv[:, :, 0], g_kc)
    v = kv[:, :, 1]
    s = jnp.einsum('bqhd,bkhd->bhqk', q, k).astype(jnp.float32)
    p = jax.nn.softmax(s, axis=-1).astype(v.dtype)
    o = jnp.einsum('bhqk,bkhd->bqhd', p, v).reshape(B, S, D_CROSS)
    return o @ wo


def hierarchical_route(h, w_group, b_group, w_router, b_router):
    T = h.shape[0]
    gl = (h @ w_group).astype(jnp.float32) + b_group.astype(jnp.float32)
    gp = jax.nn.softmax(gl, axis=-1)
    g_sel = jnp.argmax(gl, axis=-1)
    g_w = jnp.take_along_axis(gp, g_sel[:, None], axis=-1)
    el = ((h @ w_router).astype(jnp.float32) + b_router.astype(jnp.float32))
    el = el.reshape(T, N_GROUPS, EXPERTS_PER_GROUP)
    el = jnp.take_along_axis(el, g_sel[:, None, None], axis=1)[:, 0]
    top_v, top_i = lax.top_k(el, TOP_K)
    gates = (g_w * jax.nn.softmax(top_v, axis=-1)).astype(h.dtype)
    ids = g_sel[:, None].astype(jnp.int32) * EXPERTS_PER_GROUP + top_i.astype(jnp.int32)
    return ids, gates


def moe_ffn(h, ids, gates, w1, w3, w2):
    T, D = h.shape
    A = T * TOP_K
    flat_e = ids.reshape(-1)
    flat_t = jnp.repeat(jnp.arange(T, dtype=jnp.int32), TOP_K)
    flat_g = gates.reshape(-1)
    order = jnp.argsort(flat_e)
    se, st, sg = flat_e[order], flat_t[order], flat_g[order]
    counts = jnp.bincount(flat_e, length=N_EXPERTS)
    start = jnp.cumsum(counts) - counts
    pcounts = (counts + MOE_BLOCK - 1) // MOE_BLOCK * MOE_BLOCK
    pend = jnp.cumsum(pcounts)
    pstart = pend - pcounts
    dest = pstart[se] + (jnp.arange(A) - start[se])
    n_blocks = -(-(A + N_EXPERTS * (MOE_BLOCK - 1)) // MOE_BLOCK)
    P = n_blocks * MOE_BLOCK
    tok_buf = jnp.full((P,), T, dtype=jnp.int32).at[dest].set(st)
    gate_buf = jnp.zeros((P,), h.dtype).at[dest].set(sg)
    blk_expert = jnp.minimum(
        jnp.searchsorted(pend, jnp.arange(n_blocks) * MOE_BLOCK, side='right'), N_EXPERTS - 1)
    h_pad = jnp.concatenate([h, jnp.zeros((1, D), h.dtype)], axis=0)

    def body(acc, xs):
        tok, g, e = xs
        xb = h_pad[tok]
        hid = jax.nn.silu(xb @ w1[e]) * (xb @ w3[e])
        yb = (hid @ w2[e]) * g[:, None]
        return acc.at[tok].add(yb), None

    acc0 = jnp.zeros((T + 1, D), h.dtype)
    acc, _ = lax.scan(body, acc0, (tok_buf.reshape(n_blocks, MOE_BLOCK),
                                   gate_buf.reshape(n_blocks, MOE_BLOCK), blk_expert))
    return acc[:T]


def setup_inputs(seed: int = 0) -> dict:
    key = jax.random.key(seed)
    ks = list(jax.random.split(key, 40))

    def nrm(i, shape, scale):
        return jax.random.normal(ks[i], shape, jnp.float32) * scale

    def gain(i, shape):
        return 1.0 + nrm(i, shape, 0.05)

    L = DEPTH
    return {
        'x': nrm(0, (BATCH, SEQ, D_MODEL), 1.0),
        'mem': nrm(1, (BATCH, MEM_LEN, D_MODEL), 1.0),
        'rel_bias_table': nrm(2, (N_BUCKETS, N_HEADS_A), 0.2),
        'g_mix': gain(3, (L, D_MODEL)),
        'w_in': nrm(4, (L, D_MODEL, D_IN), D_MODEL ** -0.5),
        'g_q': gain(5, (L, DQ)),
        'g_k': gain(6, (L, DQ)),
        'diff_lambda': nrm(7, (L, 4, DQ), 0.1),
        'g_subln': gain(8, (L, DV)),
        'conv_w': nrm(9, (L, CONV_WIDTH, D_CONV), CONV_WIDTH ** -0.5),
        'conv_b': nrm(10, (L, D_CONV), 0.01),
        'conv_ln_g': gain(11, (L, D_CONV)),
        'conv_ln_b': nrm(12, (L, D_CONV), 0.01),
        'w_out': nrm(13, (L, D_MIX, D_MODEL), D_MIX ** -0.5),
        'g_cross': gain(14, (L, D_MODEL)),
        'g_mem': gain(15, (L, D_MODEL)),
        'wq_c': nrm(16, (L, D_MODEL, D_CROSS), D_MODEL ** -0.5),
        'wkv_c': nrm(17, (L, D_MODEL, 2 * D_CROSS), D_MODEL ** -0.5),
        'g_qc': gain(18, (L, DH_C)),
        'g_kc': gain(19, (L, DH_C)),
        'wo_c': nrm(20, (L, D_CROSS, D_MODEL), D_CROSS ** -0.5),
        'g_ffn': gain(21, (L, D_MODEL)),
        'w_group': nrm(22, (L, D_MODEL, N_GROUPS), D_MODEL ** -0.5),
        'b_group': nrm(23, (L, N_GROUPS), 0.01),
        'w_router': nrm(24, (L, D_MODEL, N_EXPERTS), D_MODEL ** -0.5),
        'b_router': nrm(25, (L, N_EXPERTS), 0.01),
        'w1': nrm(26, (L, N_EXPERTS, D_MODEL, D_EXPERT), D_MODEL ** -0.5),
        'w3': nrm(27, (L, N_EXPERTS, D_MODEL, D_EXPERT), D_MODEL ** -0.5),
        'w2': nrm(28, (L, N_EXPERTS, D_EXPERT, D_MODEL), D_EXPERT ** -0.5),
    }


def reference(x, mem, rel_bias_table, g_mix, w_in, g_q, g_k, diff_lambda, g_subln,
              conv_w, conv_b, conv_ln_g, conv_ln_b, w_out, g_cross, g_mem, wq_c, wkv_c,
              g_qc, g_kc, wo_c, g_ffn, w_group, b_group, w_router, b_router, w1, w3, w2):
    B, S, D = x.shape
    rel_bias_dist = rel_bias_table[t5_causal_bucket(jnp.arange(S))].T
    for l in range(DEPTH):
        lam_init = 0.8 - 0.6 * math.exp(-0.3 * l)
        dl = diff_lambda[l].astype(jnp.float32)
        lam = jnp.exp(jnp.sum(dl[0] * dl[1])) - jnp.exp(jnp.sum(dl[2] * dl[3])) + lam_init

        h = rmsnorm(x, g_mix[l])
        proj = h @ w_in[l]
        q = proj[..., :D_QK].reshape(B, S, N_HEADS_A, 2, DQ)
        k = proj[..., D_QK:2 * D_QK].reshape(B, S, N_HEADS_A, 2, DQ)
        v = proj[..., 2 * D_QK:2 * D_QK + N_HEADS_A * DV].reshape(B, S, N_HEADS_A, DV)
        c = proj[..., 2 * D_QK + N_HEADS_A * DV:]
        attn = diff_attention(q, k, v, lam, lam_init, g_q[l], g_k[l], g_subln[l], rel_bias_dist)
        conv = conformer_conv(c, conv_w[l], conv_b[l], conv_ln_g[l], conv_ln_b[l])
        x = x + jnp.concatenate([attn, conv], axis=-1) @ w_out[l]

        h = rmsnorm(x, g_cross[l])
        x = x + cross_attention(h, rmsnorm(mem, g_mem[l]), wq_c[l], wkv_c[l], g_qc[l], g_kc[l], wo_c[l])

        h = rmsnorm(x, g_ffn[l]).reshape(B * S, D)
        ids, gates = hierarchical_route(h, w_group[l], b_group[l], w_router[l], b_router[l])
        x = x + moe_ffn(h, ids, gates, w1[l], w3[l], w2[l]).reshape(B, S, D)
    return x
```

```python
import functools
import math

import jax
import jax.numpy as jnp
from jax import lax
from jax.experimental import pallas as pl
from jax.experimental.pallas import tpu as pltpu

DQ = 64
DV = 2 * DQ
CONV_WIDTH = 31
N_BUCKETS = 32
MAX_DISTANCE = 128
N_HEADS_C = 4
DH_C = 128
N_GROUPS = 8
EXPERTS_PER_GROUP = 8
TOP_K = 2
EPS = 1e-6
NEG_INF = -1e30

LANES = 128
MXU_DIM = 256
VMEM_LIMIT_BYTES = 56 * 1024 * 1024

F32 = jnp.float32
BF16 = jnp.bfloat16

PROJ_TM = 512
PROJ_TN = 512
ATTN_T = 256
CONV_TS = 256
CONV_HALO = 32
CROSS_TQ = 256
MOE_BLK = 256
COMB_TM = 256
TOK_BITS = 15


def _cparams(sem):
    return pltpu.CompilerParams(dimension_semantics=sem, vmem_limit_bytes=VMEM_LIMIT_BYTES)


def _sigmoid(x):
    return 1.0 / (1.0 + jnp.exp(-x))


def _norm_matmul_kernel(x_ref, g_ref, w_ref, qkg_ref, seg_ref, o_ref, xn_ref, *, n_qk_tiles):
    j = pl.program_id(1)

    @pl.when(j == 0)
    def _():
        x = x_ref[...]
        inv = lax.rsqrt(jnp.mean(x * x, axis=-1, keepdims=True) + EPS)
        xn_ref[...] = (x * inv * g_ref[...]).astype(BF16)

    acc = jnp.dot(xn_ref[...], w_ref[...], preferred_element_type=F32)

    if n_qk_tiles == 0:
        o_ref[...] = acc.astype(o_ref.dtype)
        return

    @pl.when(j < n_qk_tiles)
    def _():
        sq = (acc * acc).astype(BF16)
        seg = seg_ref[...]
        parts = [jnp.dot(sq[:, c * MXU_DIM:(c + 1) * MXU_DIM], seg, preferred_element_type=F32)
                 for c in range(acc.shape[1] // MXU_DIM)]
        ss = jnp.concatenate(parts, axis=1)
        o_ref[...] = (acc * lax.rsqrt(ss * (1.0 / DQ) + EPS) * qkg_ref[...]).astype(o_ref.dtype)

    @pl.when(j >= n_qk_tiles)
    def _():
        o_ref[...] = acc.astype(o_ref.dtype)


def _norm_matmul(x, g, w, qk_gain=None, n_qk_cols=0):
    m, d = x.shape
    n = w.shape[1]
    tm = min(PROJ_TM, m)
    tn = min(PROJ_TN, n)
    assert m % tm == 0 and n % tn == 0 and n_qk_cols % tn == 0 and tn % MXU_DIM == 0
    if qk_gain is None:
        qk_gain = jnp.ones((1, n), F32)
    r = jnp.arange(MXU_DIM) // DQ
    seg = (r[:, None] == r[None, :]).astype(BF16)
    return pl.pallas_call(
        functools.partial(_norm_matmul_kernel, n_qk_tiles=n_qk_cols // tn),
        grid=(m // tm, n // tn),
        in_specs=[
            pl.BlockSpec((tm, d), lambda i, j: (i, 0)),
            pl.BlockSpec((1, d), lambda i, j: (0, 0)),
            pl.BlockSpec((d, tn), lambda i, j: (0, j)),
            pl.BlockSpec((1, tn), lambda i, j: (0, j)),
            pl.BlockSpec((MXU_DIM, MXU_DIM), lambda i, j: (0, 0)),
        ],
        out_specs=pl.BlockSpec((tm, tn), lambda i, j: (i, j)),
        out_shape=jax.ShapeDtypeStruct((m, n), BF16),
        scratch_shapes=[pltpu.VMEM((tm, d), BF16)],
        compiler_params=_cparams(("parallel", "arbitrary")),
        name="norm_matmul",
    )(x, g.reshape(1, d), w, qk_gain, seg)


def _attn_kernel(q_ref, k_ref, v_ref, boff_ref, bdiag_ref, lam_ref, gs_ref, o_ref,
                 qs_ref, m_ref, l_ref, acc_ref, *, t):
    i = pl.program_id(2)
    q = q_ref[0].astype(F32)
    lane = lax.broadcasted_iota(jnp.int32, q.shape, 1)
    qs_ref[0:t, :] = jnp.where(lane < DQ, q, 0.0).astype(BF16)
    qs_ref[t:2 * t, :] = jnp.where(lane >= DQ, q, 0.0).astype(BF16)
    m_ref[...] = jnp.full(m_ref.shape, NEG_INF, F32)
    l_ref[...] = jnp.zeros(l_ref.shape, F32)
    acc_ref[...] = jnp.zeros(acc_ref.shape, F32)

    def step(j, bias):
        off = pl.multiple_of(j * t, t)
        k = k_ref[0, pl.ds(off, t), :]
        v = v_ref[0, pl.ds(off, t), :]
        s = lax.dot_general(qs_ref[...], k, (((1,), (1,)), ((), ())), preferred_element_type=F32)
        if bias is not None:
            s = s + bias
        m_prev = m_ref[...]
        m_new = jnp.maximum(m_prev, jnp.max(s, axis=1, keepdims=True))
        alpha = jnp.exp(m_prev - m_new)
        p = jnp.exp(s - m_new[:, 0:1])
        l_ref[...] = alpha * l_ref[...] + jnp.sum(p, axis=1, keepdims=True)
        acc_ref[...] = alpha * acc_ref[...] + jnp.dot(p.astype(BF16), v, preferred_element_type=F32)
        m_ref[...] = m_new

    def far_body(j, carry):
        step(j, None)
        return carry

    lax.fori_loop(0, i - 1, far_body, 0)

    @pl.when(i >= 1)
    def _():
        step(i - 1, boff_ref[0])

    step(i, bdiag_ref[0])

    o = acc_ref[...] / l_ref[...]
    o = o[0:t] - lam_ref[...] * o[t:2 * t]
    inv = lax.rsqrt(jnp.mean(o * o, axis=-1, keepdims=True) + EPS)
    o_ref[0] = (o * inv * gs_ref[...]).astype(o_ref.dtype)


def _diff_attention(proj, boff, bdiag, lam_row, gs_row, n_heads):
    b, s, _ = proj.shape
    t = min(ATTN_T, s)
    assert s % t == 0
    return pl.pallas_call(
        functools.partial(_attn_kernel, t=t),
        grid=(b, n_heads, s // t),
        in_specs=[
            pl.BlockSpec((1, t, DV), lambda bi, h, i: (bi, i, h)),
            pl.BlockSpec((1, s, DV), lambda bi, h, i: (bi, 0, n_heads + h)),
            pl.BlockSpec((1, s, DV), lambda bi, h, i: (bi, 0, 2 * n_heads + h)),
            pl.BlockSpec((1, 2 * t, t), lambda bi, h, i: (h, 0, 0)),
            pl.BlockSpec((1, 2 * t, t), lambda bi, h, i: (h, 0, 0)),
            pl.BlockSpec((1, DV), lambda bi, h, i: (0, 0)),
            pl.BlockSpec((1, DV), lambda bi, h, i: (0, 0)),
        ],
        out_specs=pl.BlockSpec((1, t, DV), lambda bi, h, i: (bi, i, h)),
        out_shape=jax.ShapeDtypeStruct((b, s, n_heads * DV), BF16),
        scratch_shapes=[
            pltpu.VMEM((2 * t, DV), BF16),
            pltpu.VMEM((2 * t, LANES), F32),
            pltpu.VMEM((2 * t, LANES), F32),
            pltpu.VMEM((2 * t, DV), F32),
        ],
        compiler_params=_cparams(("parallel", "parallel", "parallel")),
        name="diff_attention",
    )(proj, proj, proj, boff, bdiag, lam_row, gs_row)


def _conv_kernel(a_ref, gate_ref, w_ref, b_ref, lg_ref, lb_ref, o_ref, uext_ref, y_ref, *, ts):
    si = pl.program_id(1)

    @pl.when(si == 0)
    def _():
        uext_ref[0:CONV_HALO, :] = jnp.zeros((CONV_HALO, uext_ref.shape[1]), F32)

    @pl.when(si > 0)
    def _():
        uext_ref[0:CONV_HALO, :] = uext_ref[ts:ts + CONV_HALO, :]

    a = a_ref[0].astype(F32)
    gate = gate_ref[0].astype(F32)
    uext_ref[CONV_HALO:CONV_HALO + ts, :] = a * _sigmoid(gate)

    base = CONV_HALO - (CONV_WIDTH - 1)
    for c in range(uext_ref.shape[1] // LANES):
        cs = slice(c * LANES, (c + 1) * LANES)
        acc = jnp.broadcast_to(b_ref[:, cs], (ts, LANES))
        for j in range(CONV_WIDTH):
            acc = acc + w_ref[j:j + 1, cs] * uext_ref[base + j:base + j + ts, cs]
        y_ref[:, cs] = acc

    y = y_ref[...]
    mu = jnp.mean(y, axis=-1, keepdims=True)
    yc = y - mu
    var = jnp.mean(yc * yc, axis=-1, keepdims=True)
    z = yc * lax.rsqrt(var + EPS) * lg_ref[...] + lb_ref[...]
    o_ref[0] = (z * _sigmoid(z)).astype(o_ref.dtype)


def _conformer_conv(proj, col0, conv_w, conv_b, ln_g, ln_b):
    b, s, _ = proj.shape
    c = conv_w.shape[1]
    ts = min(CONV_TS, s)
    assert s % ts == 0 and col0 % c == 0 and ts >= CONV_HALO
    return pl.pallas_call(
        functools.partial(_conv_kernel, ts=ts),
        grid=(b, s // ts),
        in_specs=[
            pl.BlockSpec((1, ts, c), lambda bi, si: (bi, si, col0 // c)),
            pl.BlockSpec((1, ts, c), lambda bi, si: (bi, si, col0 // c + 1)),
            pl.BlockSpec((CONV_WIDTH, c), lambda bi, si: (0, 0)),
            pl.BlockSpec((1, c), lambda bi, si: (0, 0)),
            pl.BlockSpec((1, c), lambda bi, si: (0, 0)),
            pl.BlockSpec((1, c), lambda bi, si: (0, 0)),
        ],
        out_specs=pl.BlockSpec((1, ts, c), lambda bi, si: (bi, si, 0)),
        out_shape=jax.ShapeDtypeStruct((b, s, c), BF16),
        scratch_shapes=[pltpu.VMEM((CONV_HALO + ts, c), F32), pltpu.VMEM((ts, c), F32)],
        compiler_params=_cparams(("parallel", "arbitrary")),
        name="conformer_conv",
    )(proj, proj, conv_w, conv_b.reshape(1, c), ln_g.reshape(1, c), ln_b.reshape(1, c))


def _outproj_kernel(x_ref, a_ref, c_ref, w_ref, o_ref):
    ka = a_ref.shape[1]
    y = jnp.dot(a_ref[...], w_ref[0:ka, :], preferred_element_type=F32)
    y = y + jnp.dot(c_ref[...], w_ref[ka:, :], preferred_element_type=F32)
    o_ref[...] = x_ref[...] + y


def _outproj_residual(x, attn, conv, w):
    m, d = x.shape
    ka, kc = attn.shape[1], conv.shape[1]
    tm = min(PROJ_TM, m)
    tn = min(2 * PROJ_TN, d)
    assert m % tm == 0 and d % tn == 0
    return pl.pallas_call(
        _outproj_kernel,
        grid=(m // tm, d // tn),
        in_specs=[
            pl.BlockSpec((tm, tn), lambda i, j: (i, j)),
            pl.BlockSpec((tm, ka), lambda i, j: (i, 0)),
            pl.BlockSpec((tm, kc), lambda i, j: (i, 0)),
            pl.BlockSpec((ka + kc, tn), lambda i, j: (0, j)),
        ],
        out_specs=pl.BlockSpec((tm, tn), lambda i, j: (i, j)),
        out_shape=jax.ShapeDtypeStruct((m, d), F32),
        input_output_aliases={0: 0},
        compiler_params=_cparams(("parallel", "parallel")),
        name="outproj_residual",
    )(x, attn, conv, w)


def _cross_kernel(x_ref, gc_ref, wq_ref, kv_ref, gq_ref, gk_ref, wo_ref, gf_ref,
                  wrh_ref, wrl_ref, br_ref, xo_ref, hp_ref, rt_ref):
    x = x_ref[0]
    d = x.shape[1]
    tq = x.shape[0]
    h = (x * lax.rsqrt(jnp.mean(x * x, axis=-1, keepdims=True) + EPS) * gc_ref[...]).astype(BF16)
    q = jnp.dot(h, wq_ref[...], preferred_element_type=F32)
    dc = N_HEADS_C * DH_C
    outs = []
    for hd in range(N_HEADS_C):
        cs = slice(hd * DH_C, (hd + 1) * DH_C)
        qh = q[:, cs]
        qh = qh * lax.rsqrt(jnp.mean(qh * qh, axis=-1, keepdims=True) + EPS) * gq_ref[...]
        kh = kv_ref[0, :, cs].astype(F32)
        kh = kh * lax.rsqrt(jnp.mean(kh * kh, axis=-1, keepdims=True) + EPS) * gk_ref[...]
        vh = kv_ref[0, :, dc + hd * DH_C:dc + (hd + 1) * DH_C]
        s = lax.dot_general(qh.astype(BF16), kh.astype(BF16), (((1,), (1,)), ((), ())),
                            preferred_element_type=F32)
        p = jnp.exp(s - jnp.max(s, axis=-1, keepdims=True))
        p = p / jnp.sum(p, axis=-1, keepdims=True)
        outs.append(jnp.dot(p.astype(BF16), vh, preferred_element_type=F32))
    o = jnp.concatenate(outs, axis=1).astype(BF16)
    x2 = x + jnp.dot(o, wo_ref[...], preferred_element_type=F32)
    xo_ref[0] = x2

    h2 = x2 * lax.rsqrt(jnp.mean(x2 * x2, axis=-1, keepdims=True) + EPS) * gf_ref[...]
    h2_hi = h2.astype(BF16)
    h2_hi32 = h2_hi.astype(F32)
    h2_lo = (h2 - h2_hi32).astype(BF16)

    bits = lax.bitcast_convert_type(h2_hi32, jnp.uint32)
    hp_ref[0] = bits[:, 0:d // 2] | (bits[:, d // 2:] >> 16)

    nt = (((1,), (1,)), ((), ()))
    lt = (lax.dot_general(wrh_ref[...], h2_hi, nt, preferred_element_type=F32)
          + lax.dot_general(wrh_ref[...], h2_lo, nt, preferred_element_type=F32)
          + lax.dot_general(wrl_ref[...], h2_hi, nt, preferred_element_type=F32)) + br_ref[...]
    g, e = N_GROUPS, EXPERTS_PER_GROUP
    gl = lt[0:g]
    rowg = lax.broadcasted_iota(jnp.int32, gl.shape, 0)
    gmax = jnp.max(gl, axis=0, keepdims=True)
    gsel = jnp.min(jnp.where(gl == gmax, rowg, g), axis=0, keepdims=True)
    gw = 1.0 / jnp.sum(jnp.exp(gl - gmax), axis=0, keepdims=True)
    el = jnp.zeros((e, tq), F32)
    for gi in range(g):
        el = jnp.where(gsel == gi, lt[g + gi * e:g + (gi + 1) * e], el)
    rowe = lax.broadcasted_iota(jnp.int32, el.shape, 0)
    v1 = jnp.max(el, axis=0, keepdims=True)
    i1 = jnp.min(jnp.where(el == v1, rowe, e), axis=0, keepdims=True)
    el2 = jnp.where(rowe == i1, -jnp.inf, el)
    v2 = jnp.max(el2, axis=0, keepdims=True)
    i2 = jnp.min(jnp.where(el2 == v2, rowe, e), axis=0, keepdims=True)
    e2 = jnp.exp(v2 - v1)
    den = 1.0 / (1.0 + e2)
    id1 = (gsel * e + i1).astype(F32)
    id2 = (gsel * e + i2).astype(F32)
    row8 = lax.broadcasted_iota(jnp.int32, (8, tq), 0)
    rt = jnp.where(row8 == 0, id1, jnp.where(row8 == 1, id2,
         jnp.where(row8 == 2, gw * den, jnp.where(row8 == 3, gw * e2 * den, 0.0))))
    rt_ref[0] = rt


def _cross_block(x, kv, g_cross, wq, g_qc, g_kc, wo, g_ffn, wr_hi, wr_lo, b_r):
    b, s, d = x.shape
    mlen = kv.shape[1]
    tq = min(CROSS_TQ, s)
    nq = s // tq
    dc = N_HEADS_C * DH_C
    nr = wr_hi.shape[0]
    assert s % tq == 0
    const = lambda bi, qi: (0, 0)
    return pl.pallas_call(
        _cross_kernel,
        grid=(b, nq),
        in_specs=[
            pl.BlockSpec((1, tq, d), lambda bi, qi: (bi, qi, 0)),
            pl.BlockSpec((1, d), const),
            pl.BlockSpec((d, dc), const),
            pl.BlockSpec((1, mlen, 2 * dc), lambda bi, qi: (bi, 0, 0)),
            pl.BlockSpec((1, DH_C), const),
            pl.BlockSpec((1, DH_C), const),
            pl.BlockSpec((dc, d), const),
            pl.BlockSpec((1, d), const),
            pl.BlockSpec((nr, d), const),
            pl.BlockSpec((nr, d), const),
            pl.BlockSpec((nr, 1), const),
        ],
        out_specs=[
            pl.BlockSpec((1, tq, d), lambda bi, qi: (bi, qi, 0)),
            pl.BlockSpec((1, tq, d // 2), lambda bi, qi: (bi, qi, 0)),
            pl.BlockSpec((1, 8, tq), lambda bi, qi: (bi * nq + qi, 0, 0)),
        ],
        out_shape=[
            jax.ShapeDtypeStruct((b, s, d), F32),
            jax.ShapeDtypeStruct((b, s, d // 2), jnp.uint32),
            jax.ShapeDtypeStruct((b * nq, 8, tq), F32),
        ],
        input_output_aliases={0: 0},
        compiler_params=_cparams(("parallel", "parallel")),
        name="cross_block",
    )(x, g_cross.reshape(1, d), wq, kv, (g_qc * DH_C ** -0.5).reshape(1, DH_C), g_kc.reshape(1, DH_C),
      wo, g_ffn.reshape(1, d), wr_hi, wr_lo, b_r)


def _expert_kernel(idx_ref, be_ref, nr_ref, h_hbm, w1_ref, w3_ref, w2_ref, g_ref, z_hbm,
                   xbuf, ybuf, gsem, ssem, *, blk, n_steps):
    i = pl.program_id(0)
    n_cur = nr_ref[i]
    n_next = nr_ref[i + 1]
    slot = i % 2
    tok_mask = (1 << TOK_BITS) - 1

    def issue_gather(step, sl):
        def body(r, carry):
            row = idx_ref[step * blk + r] & tok_mask
            pltpu.make_async_copy(h_hbm.at[pl.ds(row, 1)], xbuf.at[sl, pl.ds(r, 1)], gsem.at[sl]).start()
            return carry
        lax.fori_loop(0, blk, body, 0, unroll=8)

    def wait_gather(sl):
        pltpu.make_async_copy(h_hbm.at[pl.ds(0, blk)], xbuf.at[sl], gsem.at[sl]).wait()

    def issue_scatter(step, sl, n):
        def body(r, carry):
            row = lax.shift_right_logical(idx_ref[step * blk + r], TOK_BITS)

            @pl.when(r < n)
            def _():
                pltpu.make_async_copy(ybuf.at[sl, pl.ds(r, 1)], z_hbm.at[pl.ds(row, 1)], ssem.at[sl]).start()
            return carry
        lax.fori_loop(0, blk, body, 0, unroll=8)

    def wait_scatter(sl, n):
        k = blk
        while k >= 1:
            @pl.when((n & k) != 0)
            def _(k=k):
                pltpu.make_async_copy(ybuf.at[sl, pl.ds(0, k)], z_hbm.at[pl.ds(0, k)], ssem.at[sl]).wait()
            k //= 2

    @pl.when(i == 0)
    def _():
        issue_gather(0, 0)

    @pl.when(n_next > 0)
    def _():
        issue_gather(i + 1, 1 - slot)

    @pl.when(n_cur > 0)
    def _():
        wait_gather(slot)
        w = xbuf[slot]
        dh = w.shape[1]
        xl = lax.bitcast_convert_type(w & jnp.uint32(0xFFFF0000), F32).astype(BF16)
        xr = lax.bitcast_convert_type(w << 16, F32).astype(BF16)
        a1 = (jnp.dot(xl, w1_ref[0, 0:dh, :], preferred_element_type=F32)
              + jnp.dot(xr, w1_ref[0, dh:, :], preferred_element_type=F32))
        a3 = (jnp.dot(xl, w3_ref[0, 0:dh, :], preferred_element_type=F32)
              + jnp.dot(xr, w3_ref[0, dh:, :], preferred_element_type=F32))
        hid = (a1 * _sigmoid(a1) * a3 * g_ref[...]).astype(BF16)
        ybuf[slot] = jnp.dot(hid, w2_ref[0], preferred_element_type=F32)
        issue_scatter(i, slot, n_cur)

    @pl.when(i >= 1)
    def _():
        n_prev = nr_ref[jnp.maximum(i - 1, 0)]

        @pl.when(n_prev > 0)
        def _():
            wait_scatter(1 - slot, n_prev)

    @pl.when((i == n_steps - 1) & (n_cur > 0))
    def _():
        wait_scatter(slot, n_cur)


def _expert_mlp(hpack, idx, blk_expert, nreal, gate_buf, w1, w3, w2, z_rows):
    t, dh = hpack.shape
    d = 2 * dh
    p = idx.shape[0]
    n_steps = p // MOE_BLK
    de = w1.shape[2]
    assert MOE_BLK & (MOE_BLK - 1) == 0
    return pl.pallas_call(
        functools.partial(_expert_kernel, blk=MOE_BLK, n_steps=n_steps),
        grid_spec=pltpu.PrefetchScalarGridSpec(
            num_scalar_prefetch=3,
            grid=(n_steps,),
            in_specs=[
                pl.BlockSpec(memory_space=pl.ANY),
                pl.BlockSpec((1, d, de), lambda i, ix, be, nv: (be[i], 0, 0)),
                pl.BlockSpec((1, d, de), lambda i, ix, be, nv: (be[i], 0, 0)),
                pl.BlockSpec((1, de, d), lambda i, ix, be, nv: (be[i], 0, 0)),
                pl.BlockSpec((MOE_BLK, 1), lambda i, ix, be, nv: (i, 0)),
            ],
            out_specs=pl.BlockSpec(memory_space=pl.ANY),
            scratch_shapes=[
                pltpu.VMEM((2, MOE_BLK, dh), jnp.uint32),
                pltpu.VMEM((2, MOE_BLK, d), F32),
                pltpu.SemaphoreType.DMA((2,)),
                pltpu.SemaphoreType.DMA((2,)),
            ],
        ),
        out_shape=jax.ShapeDtypeStruct((z_rows, d), F32),
        compiler_params=_cparams(("arbitrary",)),
        name="expert_mlp",
    )(idx, blk_expert, nreal, hpack, w1, w3, w2, gate_buf)


def _combine_kernel(x_ref, z_ref, o_ref):
    acc = x_ref[...]
    for k in range(z_ref.shape[0]):
        acc = acc + z_ref[k]
    o_ref[...] = acc


def _combine(x, z3):
    t, d = x.shape
    tm = min(COMB_TM, t)
    assert t % tm == 0
    return pl.pallas_call(
        _combine_kernel,
        grid=(t // tm,),
        in_specs=[
            pl.BlockSpec((tm, d), lambda i: (i, 0)),
            pl.BlockSpec((z3.shape[0], tm, d), lambda i: (0, i, 0)),
        ],
        out_specs=pl.BlockSpec((tm, d), lambda i: (i, 0)),
        out_shape=jax.ShapeDtypeStruct((t, d), F32),
        input_output_aliases={0: 0},
        compiler_params=_cparams(("parallel",)),
        name="moe_combine",
    )(x, z3)


def _t5_causal_bucket(dist):
    max_exact = N_BUCKETS // 2
    d_f = jnp.maximum(dist, 1).astype(F32)
    large = max_exact + (jnp.log(d_f / max_exact) / math.log(MAX_DISTANCE / max_exact)
                         * (N_BUCKETS - max_exact)).astype(jnp.int32)
    large = jnp.minimum(large, N_BUCKETS - 1)
    return jnp.where(dist < max_exact, dist, large)


def _bias_tiles(rel_bias_table, s, t):
    assert t >= MAX_DISTANCE
    rbd = rel_bias_table[_t5_causal_bucket(jnp.arange(s))].T.astype(F32)
    far = rel_bias_table[N_BUCKETS - 1].astype(F32)
    r = jnp.arange(t)[:, None]
    c = jnp.arange(t)[None, :]
    dd = r - c
    diag = jnp.where(dd >= 0, rbd[:, jnp.maximum(dd, 0)] - far[:, None, None], NEG_INF)
    off = rbd[:, jnp.minimum(dd + t, s - 1)] - far[:, None, None]
    return jnp.concatenate([off, off], axis=1), jnp.concatenate([diag, diag], axis=1)


def _moe_plan(rt, t, n_experts, blk):
    nblk_rt, _, tq = rt.shape
    ids = rt[:, 0:TOP_K, :].astype(jnp.int32).transpose(0, 2, 1).reshape(t * TOP_K)
    gates = rt[:, TOP_K:2 * TOP_K, :].transpose(0, 2, 1).reshape(t * TOP_K)
    a = t * TOP_K
    order = jnp.argsort(ids, stable=True).astype(jnp.int32)
    se = ids[order]
    counts = jnp.bincount(ids, length=n_experts)
    start = jnp.cumsum(counts) - counts
    pcounts = (counts + blk - 1) // blk * blk
    pend = jnp.cumsum(pcounts)
    pstart = pend - pcounts
    dest = (pstart[se] + (jnp.arange(a) - start[se])).astype(jnp.int32)
    n_blocks = -(-(a + n_experts * (blk - 1)) // blk)
    p = n_blocks * blk
    order_u = order.astype(jnp.uint32)
    zrow = (order_u % TOP_K) * t + order_u // TOP_K
    idx = jnp.zeros((p,), jnp.uint32).at[dest].set((zrow << TOK_BITS) | (order_u // TOP_K))
    idx = lax.bitcast_convert_type(idx, jnp.int32)
    gate_buf = jnp.zeros((p,), F32).at[dest].set(gates[order])
    blk_expert = jnp.minimum(
        jnp.searchsorted(pend, jnp.arange(n_blocks) * blk, side='right'), n_experts - 1).astype(jnp.int32)
    real_end = (pstart + counts)[blk_expert]
    nreal = jnp.clip(real_end - jnp.arange(n_blocks) * blk, 0, blk).astype(jnp.int32)
    nreal = jnp.concatenate([nreal, jnp.zeros((1,), jnp.int32)])
    return idx, blk_expert, nreal, gate_buf.reshape(p, 1)


def kernel(x, mem, rel_bias_table, g_mix, w_in, g_q, g_k, diff_lambda, g_subln, conv_w, conv_b,
           conv_ln_g, conv_ln_b, w_out, g_cross, g_mem, wq_c, wkv_c, g_qc, g_kc, wo_c, g_ffn,
           w_group, b_group, w_router, b_router, w1, w3, w2):
    b, s, d = x.shape
    mlen = mem.shape[1]
    depth = w_in.shape[0]
    t = b * s
    d_conv = conv_w.shape[2]
    n_heads = (w_out.shape[1] - d_conv) // DV
    d_qk = n_heads * 2 * DQ
    n_experts = w_router.shape[2]
    assert t <= (1 << TOK_BITS) and TOP_K * t <= (1 << (32 - TOK_BITS))

    boff, bdiag = _bias_tiles(rel_bias_table, s, min(ATTN_T, s))
    n_router_rows = -(-(N_GROUPS + n_experts) // LANES) * LANES

    for l in range(depth):
        lam_init = 0.8 - 0.6 * math.exp(-0.3 * l)
        dl = diff_lambda[l].astype(F32)
        lam = jnp.exp(jnp.sum(dl[0] * dl[1])) - jnp.exp(jnp.sum(dl[2] * dl[3])) + lam_init
        lam_row = jnp.full((1, DV), lam, F32)
        gs_row = (g_subln[l] * (1.0 - lam_init)).reshape(1, DV)
        qk_gain = jnp.concatenate([
            jnp.tile(g_q[l] * DQ ** -0.5, d_qk // DQ), jnp.tile(g_k[l], d_qk // DQ),
            jnp.ones((w_in.shape[2] - 2 * d_qk,), F32)]).reshape(1, -1)

        proj = _norm_matmul(x.reshape(t, d), g_mix[l], w_in[l].astype(BF16), qk_gain, 2 * d_qk)
        proj = proj.reshape(b, s, -1)
        attn = _diff_attention(proj, boff, bdiag, lam_row, gs_row, n_heads)
        conv = _conformer_conv(proj, 2 * d_qk + n_heads * DV, conv_w[l], conv_b[l],
                               conv_ln_g[l], conv_ln_b[l])
        x = _outproj_residual(x.reshape(t, d), attn.reshape(t, -1), conv.reshape(t, -1),
                              w_out[l].astype(BF16)).reshape(b, s, d)

        kv = _norm_matmul(mem.reshape(b * mlen, d), g_mem[l], wkv_c[l].astype(BF16))
        wr = jnp.zeros((n_router_rows, d), F32)
        wr = wr.at[0:N_GROUPS].set(w_group[l].T).at[N_GROUPS:N_GROUPS + n_experts].set(w_router[l].T)
        wr_hi = wr.astype(BF16)
        wr_lo = (wr - wr_hi.astype(F32)).astype(BF16)
        b_r = jnp.zeros((n_router_rows, 1), F32)
        b_r = b_r.at[0:N_GROUPS, 0].set(b_group[l]).at[N_GROUPS:N_GROUPS + n_experts, 0].set(b_router[l])
        x, hpack, rt = _cross_block(x, kv.reshape(b, mlen, -1), g_cross[l], wq_c[l].astype(BF16),
                                    g_qc[l], g_kc[l], wo_c[l].astype(BF16), g_ffn[l], wr_hi, wr_lo, b_r)

        idx, blk_expert, nreal, gate_buf = _moe_plan(rt, t, n_experts, MOE_BLK)
        z = _expert_mlp(hpack.reshape(t, d // 2), idx, blk_expert, nreal, gate_buf,
                        w1[l].astype(BF16), w3[l].astype(BF16), w2[l].astype(BF16), TOP_K * t)
        x = _combine(x.reshape(t, d), z.reshape(TOP_K, t, d)).reshape(b, s, d)
    return x
```

```python
import functools
import math

import jax
import jax.numpy as jnp
import numpy as np
from jax import lax
from jax.experimental import pallas as pl
from jax.experimental.pallas import tpu as pltpu

DQ = 64
DV = 2 * DQ
CONV_WIDTH = 31
N_BUCKETS = 32
MAX_DISTANCE = 128
N_HEADS_C = 4
DH_C = 128
N_GROUPS = 8
EXPERTS_PER_GROUP = 8
TOP_K = 2
EPS = 1e-6
NEG_INF = -1e30
LOG2E = 1.4426950408889634

LANES = 128
MXU_DIM = 256
VMEM_LIMIT_BYTES = 56 * 1024 * 1024

F32 = jnp.float32
BF16 = jnp.bfloat16

PROJ_TM = 512
PROJ_TN = 512
ATTN_T = 256
CONV_TS = 256
CONV_HALO = 32
CROSS_TQ = 256
MOE_BLK = 256
COMB_TM = 256
TOK_BITS = 15


def _cparams(sem):
    return pltpu.CompilerParams(dimension_semantics=sem, vmem_limit_bytes=VMEM_LIMIT_BYTES)


def _sigmoid(x):
    return 1.0 / (1.0 + jnp.exp(-x))


def _norm_matmul_kernel(x_ref, g_ref, w_ref, qkg_ref, seg_ref, o_ref, xn_ref, *, n_qk_tiles):
    j = pl.program_id(1)

    @pl.when(j == 0)
    def _():
        x = x_ref[...]
        inv = lax.rsqrt(jnp.mean(x * x, axis=-1, keepdims=True) + EPS)
        xn_ref[...] = (x * inv * g_ref[...]).astype(BF16)

    acc = jnp.dot(xn_ref[...], w_ref[...], preferred_element_type=F32)

    if n_qk_tiles == 0:
        o_ref[...] = acc.astype(o_ref.dtype)
        return

    @pl.when(j < n_qk_tiles)
    def _():
        sq = (acc * acc).astype(BF16)
        seg = seg_ref[...]
        parts = [jnp.dot(sq[:, c * MXU_DIM:(c + 1) * MXU_DIM], seg, preferred_element_type=F32)
                 for c in range(acc.shape[1] // MXU_DIM)]
        ss = jnp.concatenate(parts, axis=1)
        o_ref[...] = (acc * lax.rsqrt(ss * (1.0 / DQ) + EPS) * qkg_ref[...]).astype(o_ref.dtype)

    @pl.when(j >= n_qk_tiles)
    def _():
        o_ref[...] = acc.astype(o_ref.dtype)


def _norm_matmul(x, g, w, qk_gain=None, n_qk_cols=0):
    m, d = x.shape
    n = w.shape[1]
    tm = min(PROJ_TM, m)
    tn = min(PROJ_TN, n)
    assert m % tm == 0 and n % tn == 0 and n_qk_cols % tn == 0 and tn % MXU_DIM == 0
    if qk_gain is None:
        qk_gain = jnp.ones((1, n), F32)
    r = jnp.arange(MXU_DIM) // DQ
    seg = (r[:, None] == r[None, :]).astype(BF16)
    return pl.pallas_call(
        functools.partial(_norm_matmul_kernel, n_qk_tiles=n_qk_cols // tn),
        grid=(m // tm, n // tn),
        in_specs=[
            pl.BlockSpec((tm, d), lambda i, j: (i, 0)),
            pl.BlockSpec((1, d), lambda i, j: (0, 0)),
            pl.BlockSpec((d, tn), lambda i, j: (0, j)),
            pl.BlockSpec((1, tn), lambda i, j: (0, j)),
            pl.BlockSpec((MXU_DIM, MXU_DIM), lambda i, j: (0, 0)),
        ],
        out_specs=pl.BlockSpec((tm, tn), lambda i, j: (i, j)),
        out_shape=jax.ShapeDtypeStruct((m, n), BF16),
        scratch_shapes=[pltpu.VMEM((tm, d), BF16)],
        compiler_params=_cparams(("parallel", "arbitrary")),
        name="norm_matmul",
    )(x, g.reshape(1, d), w, qk_gain, seg)


def _attn_kernel(q_ref, k_ref, v_ref, boff_ref, bdiag_ref, lam_ref, gs_ref, o_ref,
                 qs_ref, vt_ref, m_ref, l_ref, acc_ref, *, t):
    i = pl.program_id(2)
    s_len = k_ref.shape[1]

    @pl.when(i == 0)
    def _():
        for c in range(s_len // t):
            vt_ref[c] = v_ref[0, c * t:(c + 1) * t, :].astype(F32).T.astype(BF16)

    q = q_ref[0].astype(F32)
    lane = lax.broadcasted_iota(jnp.int32, q.shape, 1)
    qs_ref[0:t, :] = jnp.where(lane < DQ, q, 0.0).astype(BF16)
    qs_ref[t:2 * t, :] = jnp.where(lane >= DQ, q, 0.0).astype(BF16)
    m_ref[...] = jnp.full(m_ref.shape, NEG_INF, F32)
    l_ref[...] = jnp.zeros(l_ref.shape, F32)
    acc_ref[...] = jnp.zeros(acc_ref.shape, F32)

    def step(j, bias):
        off = pl.multiple_of(j * t, t)
        k = k_ref[0, pl.ds(off, t), :]
        st = lax.dot_general(k, qs_ref[...], (((1,), (1,)), ((), ())), preferred_element_type=F32)
        if bias is not None:
            st = st + bias
        m_prev = m_ref[...]
        m_new = jnp.maximum(m_prev, jnp.max(st, axis=0, keepdims=True))
        alpha = jnp.exp2(m_prev - m_new)
        pt = jnp.exp2(st - m_new)
        l_ref[...] = alpha * l_ref[...] + jnp.sum(pt, axis=0, keepdims=True)
        pv = jnp.dot(vt_ref[j], pt.astype(BF16), preferred_element_type=F32)
        acc_ref[...] = alpha * acc_ref[...] + pv
        m_ref[...] = m_new

    def far_body(j, carry):
        step(j, None)
        return carry

    lax.fori_loop(0, i - 1, far_body, 0)

    @pl.when(i >= 1)
    def _():
        step(i - 1, boff_ref[0])

    step(i, bdiag_ref[0])

    ot = acc_ref[...] / l_ref[...]
    ot = ot[:, 0:t] - lam_ref[...] * ot[:, t:2 * t]
    inv = lax.rsqrt(jnp.mean(ot * ot, axis=0, keepdims=True) + EPS)
    o_ref[0] = (ot * inv * gs_ref[...]).T.astype(o_ref.dtype)


def _diff_attention(proj, boff, bdiag, lam, gs, n_heads):
    b, s, _ = proj.shape
    t = boff.shape[1]
    assert s % t == 0
    lam_row = jnp.full((1, t), lam, F32)
    gs_col = jnp.broadcast_to(gs.astype(F32)[:, None], (DV, t))
    return pl.pallas_call(
        functools.partial(_attn_kernel, t=t),
        grid=(b, n_heads, s // t),
        in_specs=[
            pl.BlockSpec((1, t, DV), lambda bi, h, i: (bi, i, h)),
            pl.BlockSpec((1, s, DV), lambda bi, h, i: (bi, 0, n_heads + h)),
            pl.BlockSpec((1, s, DV), lambda bi, h, i: (bi, 0, 2 * n_heads + h)),
            pl.BlockSpec((1, t, 2 * t), lambda bi, h, i: (h, 0, 0)),
            pl.BlockSpec((1, t, 2 * t), lambda bi, h, i: (h, 0, 0)),
            pl.BlockSpec((1, t), lambda bi, h, i: (0, 0)),
            pl.BlockSpec((DV, t), lambda bi, h, i: (0, 0)),
        ],
        out_specs=pl.BlockSpec((1, t, DV), lambda bi, h, i: (bi, i, h)),
        out_shape=jax.ShapeDtypeStruct((b, s, n_heads * DV), BF16),
        scratch_shapes=[
            pltpu.VMEM((2 * t, DV), BF16),
            pltpu.VMEM((s // t, DV, t), BF16),
            pltpu.VMEM((1, 2 * t), F32),
            pltpu.VMEM((1, 2 * t), F32),
            pltpu.VMEM((DV, 2 * t), F32),
        ],
        compiler_params=_cparams(("parallel", "parallel", "arbitrary")),
        name="diff_attention",
    )(proj, proj, proj, boff, bdiag, lam_row, gs_col)


def _conv_kernel(a_ref, gate_ref, w_ref, b_ref, lg_ref, lb_ref, o_ref, uext_ref, y_ref, *, ts):
    si = pl.program_id(1)

    @pl.when(si == 0)
    def _():
        uext_ref[0:CONV_HALO, :] = jnp.zeros((CONV_HALO, uext_ref.shape[1]), F32)

    @pl.when(si > 0)
    def _():
        uext_ref[0:CONV_HALO, :] = uext_ref[ts:ts + CONV_HALO, :]

    a = a_ref[0].astype(F32)
    gate = gate_ref[0].astype(F32)
    uext_ref[CONV_HALO:CONV_HALO + ts, :] = a * _sigmoid(gate)

    base = CONV_HALO - (CONV_WIDTH - 1)
    for c in range(uext_ref.shape[1] // LANES):
        cs = slice(c * LANES, (c + 1) * LANES)
        acc = jnp.broadcast_to(b_ref[:, cs], (ts, LANES))
        for j in range(CONV_WIDTH):
            acc = acc + w_ref[j:j + 1, cs] * uext_ref[base + j:base + j + ts, cs]
        y_ref[:, cs] = acc

    y = y_ref[...]
    mu = jnp.mean(y, axis=-1, keepdims=True)
    yc = y - mu
    var = jnp.mean(yc * yc, axis=-1, keepdims=True)
    z = yc * lax.rsqrt(var + EPS) * lg_ref[...] + lb_ref[...]
    o_ref[0] = (z * _sigmoid(z)).astype(o_ref.dtype)


def _conformer_conv(proj, col0, conv_w, conv_b, ln_g, ln_b):
    b, s, _ = proj.shape
    c = conv_w.shape[1]
    ts = min(CONV_TS, s)
    assert s % ts == 0 and col0 % c == 0 and ts >= CONV_HALO
    return pl.pallas_call(
        functools.partial(_conv_kernel, ts=ts),
        grid=(b, s // ts),
        in_specs=[
            pl.BlockSpec((1, ts, c), lambda bi, si: (bi, si, col0 // c)),
            pl.BlockSpec((1, ts, c), lambda bi, si: (bi, si, col0 // c + 1)),
            pl.BlockSpec((CONV_WIDTH, c), lambda bi, si: (0, 0)),
            pl.BlockSpec((1, c), lambda bi, si: (0, 0)),
            pl.BlockSpec((1, c), lambda bi, si: (0, 0)),
            pl.BlockSpec((1, c), lambda bi, si: (0, 0)),
        ],
        out_specs=pl.BlockSpec((1, ts, c), lambda bi, si: (bi, si, 0)),
        out_shape=jax.ShapeDtypeStruct((b, s, c), BF16),
        scratch_shapes=[pltpu.VMEM((CONV_HALO + ts, c), F32), pltpu.VMEM((ts, c), F32)],
        compiler_params=_cparams(("parallel", "arbitrary")),
        name="conformer_conv",
    )(proj, proj, conv_w, conv_b.reshape(1, c), ln_g.reshape(1, c), ln_b.reshape(1, c))


def _outproj_kernel(x_ref, a_ref, c_ref, w_ref, o_ref):
    ka = a_ref.shape[1]
    y = jnp.dot(a_ref[...], w_ref[0:ka, :], preferred_element_type=F32)
    y = y + jnp.dot(c_ref[...], w_ref[ka:, :], preferred_element_type=F32)
    o_ref[...] = x_ref[...] + y


def _outproj_residual(x, attn, conv, w, in_place):
    m, d = x.shape
    ka, kc = attn.shape[1], conv.shape[1]
    tm = min(PROJ_TM, m)
    tn = min(2 * PROJ_TN, d)
    assert m % tm == 0 and d % tn == 0
    return pl.pallas_call(
        _outproj_kernel,
        grid=(m // tm, d // tn),
        in_specs=[
            pl.BlockSpec((tm, tn), lambda i, j: (i, j)),
            pl.BlockSpec((tm, ka), lambda i, j: (i, 0)),
            pl.BlockSpec((tm, kc), lambda i, j: (i, 0)),
            pl.BlockSpec((ka + kc, tn), lambda i, j: (0, j)),
        ],
        out_specs=pl.BlockSpec((tm, tn), lambda i, j: (i, j)),
        out_shape=jax.ShapeDtypeStruct((m, d), F32),
        input_output_aliases={0: 0} if in_place else {},
        compiler_params=_cparams(("parallel", "parallel")),
        name="outproj_residual",
    )(x, attn, conv, w)


def _cross_kernel(x_ref, gc_ref, wq_ref, kv_ref, gq_ref, gk_ref, wo_ref, gf_ref,
                  wrh_ref, wrl_ref, br_ref, xo_ref, hp_ref, rt_ref):
    x = x_ref[0]
    d = x.shape[1]
    tq = x.shape[0]
    h = (x * lax.rsqrt(jnp.mean(x * x, axis=-1, keepdims=True) + EPS) * gc_ref[...]).astype(BF16)
    q = jnp.dot(h, wq_ref[...], preferred_element_type=F32)
    dc = N_HEADS_C * DH_C
    outs = []
    for hd in range(N_HEADS_C):
        cs = slice(hd * DH_C, (hd + 1) * DH_C)
        qh = q[:, cs]
        qh = qh * lax.rsqrt(jnp.mean(qh * qh, axis=-1, keepdims=True) + EPS) * gq_ref[...]
        kh = kv_ref[0, :, cs].astype(F32)
        kh = kh * lax.rsqrt(jnp.mean(kh * kh, axis=-1, keepdims=True) + EPS) * gk_ref[...]
        vh = kv_ref[0, :, dc + hd * DH_C:dc + (hd + 1) * DH_C]
        s = lax.dot_general(qh.astype(BF16), kh.astype(BF16), (((1,), (1,)), ((), ())),
                            preferred_element_type=F32)
        p = jnp.exp(s - jnp.max(s, axis=-1, keepdims=True))
        p = p / jnp.sum(p, axis=-1, keepdims=True)
        outs.append(jnp.dot(p.astype(BF16), vh, preferred_element_type=F32))
    o = jnp.concatenate(outs, axis=1).astype(BF16)
    x2 = x + jnp.dot(o, wo_ref[...], preferred_element_type=F32)
    xo_ref[0] = x2

    h2 = x2 * lax.rsqrt(jnp.mean(x2 * x2, axis=-1, keepdims=True) + EPS) * gf_ref[...]
    h2_hi = h2.astype(BF16)
    h2_hi32 = h2_hi.astype(F32)
    h2_lo = (h2 - h2_hi32).astype(BF16)

    bits = lax.bitcast_convert_type(h2_hi32, jnp.uint32)
    hp_ref[0] = bits[:, 0:d // 2] | (bits[:, d // 2:] >> 16)

    nt = (((1,), (1,)), ((), ()))
    lt = (lax.dot_general(wrh_ref[...], h2_hi, nt, preferred_element_type=F32)
          + lax.dot_general(wrh_ref[...], h2_lo, nt, preferred_element_type=F32)
          + lax.dot_general(wrl_ref[...], h2_hi, nt, preferred_element_type=F32)) + br_ref[...]
    g, e = N_GROUPS, EXPERTS_PER_GROUP
    gl = lt[0:g]
    rowg = lax.broadcasted_iota(jnp.int32, gl.shape, 0)
    gmax = jnp.max(gl, axis=0, keepdims=True)
    gsel = jnp.min(jnp.where(gl == gmax, rowg, g), axis=0, keepdims=True)
    gw = 1.0 / jnp.sum(jnp.exp(gl - gmax), axis=0, keepdims=True)
    el = jnp.zeros((e, tq), F32)
    for gi in range(g):
        el = jnp.where(gsel == gi, lt[g + gi * e:g + (gi + 1) * e], el)
    rowe = lax.broadcasted_iota(jnp.int32, el.shape, 0)
    v1 = jnp.max(el, axis=0, keepdims=True)
    i1 = jnp.min(jnp.where(el == v1, rowe, e), axis=0, keepdims=True)
    el2 = jnp.where(rowe == i1, -jnp.inf, el)
    v2 = jnp.max(el2, axis=0, keepdims=True)
    i2 = jnp.min(jnp.where(el2 == v2, rowe, e), axis=0, keepdims=True)
    e2 = jnp.exp(v2 - v1)
    den = 1.0 / (1.0 + e2)
    id1 = (gsel * e + i1).astype(F32)
    id2 = (gsel * e + i2).astype(F32)
    row8 = lax.broadcasted_iota(jnp.int32, (8, tq), 0)
    rt = jnp.where(row8 == 0, id1, jnp.where(row8 == 1, id2,
         jnp.where(row8 == 2, gw * den, jnp.where(row8 == 3, gw * e2 * den, 0.0))))
    rt_ref[0] = rt


def _cross_block(x, kv, g_cross, wq, g_qc, g_kc, wo, g_ffn, wr_hi, wr_lo, b_r):
    b, s, d = x.shape
    mlen = kv.shape[1]
    tq = min(CROSS_TQ, s)
    nq = s // tq
    dc = N_HEADS_C * DH_C
    nr = wr_hi.shape[0]
    assert s % tq == 0
    const = lambda bi, qi: (0, 0)
    return pl.pallas_call(
        _cross_kernel,
        grid=(b, nq),
        in_specs=[
            pl.BlockSpec((1, tq, d), lambda bi, qi: (bi, qi, 0)),
            pl.BlockSpec((1, d), const),
            pl.BlockSpec((d, dc), const),
            pl.BlockSpec((1, mlen, 2 * dc), lambda bi, qi: (bi, 0, 0)),
            pl.BlockSpec((1, DH_C), const),
            pl.BlockSpec((1, DH_C), const),
            pl.BlockSpec((dc, d), const),
            pl.BlockSpec((1, d), const),
            pl.BlockSpec((nr, d), const),
            pl.BlockSpec((nr, d), const),
            pl.BlockSpec((nr, 1), const),
        ],
        out_specs=[
            pl.BlockSpec((1, tq, d), lambda bi, qi: (bi, qi, 0)),
            pl.BlockSpec((1, tq, d // 2), lambda bi, qi: (bi, qi, 0)),
            pl.BlockSpec((1, 8, tq), lambda bi, qi: (bi * nq + qi, 0, 0)),
        ],
        out_shape=[
            jax.ShapeDtypeStruct((b, s, d), F32),
            jax.ShapeDtypeStruct((b, s, d // 2), jnp.uint32),
            jax.ShapeDtypeStruct((b * nq, 8, tq), F32),
        ],
        input_output_aliases={0: 0},
        compiler_params=_cparams(("parallel", "parallel")),
        name="cross_block",
    )(x, g_cross.reshape(1, d), wq, kv, (g_qc * DH_C ** -0.5).reshape(1, DH_C), g_kc.reshape(1, DH_C),
      wo, g_ffn.reshape(1, d), wr_hi, wr_lo, b_r)


def _expert_kernel(idx_ref, be_ref, nr_ref, h_hbm, w1_ref, w3_ref, w2_ref, g_ref, z_hbm,
                   xbuf, ybuf, w1b, w3b, w2b, gsem, ssem, *, blk, n_steps):
    i = pl.program_id(0)
    n_cur = nr_ref[i]
    n_next = nr_ref[i + 1]
    slot = i % 2
    tok_mask = (1 << TOK_BITS) - 1

    def issue_gather(step, sl):
        def body(r, carry):
            row = idx_ref[step * blk + r] & tok_mask
            pltpu.make_async_copy(h_hbm.at[pl.ds(row, 1)], xbuf.at[sl, pl.ds(r, 1)], gsem.at[sl]).start()
            return carry
        lax.fori_loop(0, blk, body, 0, unroll=8)

    def wait_gather(sl):
        pltpu.make_async_copy(h_hbm.at[pl.ds(0, blk)], xbuf.at[sl], gsem.at[sl]).wait()

    def issue_scatter(step, sl, n):
        def body(r, carry):
            row = lax.shift_right_logical(idx_ref[step * blk + r], TOK_BITS)

            @pl.when(r < n)
            def _():
                pltpu.make_async_copy(ybuf.at[sl, pl.ds(r, 1)], z_hbm.at[pl.ds(row, 1)], ssem.at[sl]).start()
            return carry
        lax.fori_loop(0, blk, body, 0, unroll=8)

    def wait_scatter(sl, n):
        k = blk
        while k >= 1:
            @pl.when((n & k) != 0)
            def _(k=k):
                pltpu.make_async_copy(ybuf.at[sl, pl.ds(0, k)], z_hbm.at[pl.ds(0, k)], ssem.at[sl]).wait()
            k //= 2

    @pl.when(i == 0)
    def _():
        issue_gather(0, 0)

    @pl.when(n_next > 0)
    def _():
        issue_gather(i + 1, 1 - slot)

    @pl.when((n_cur > 0) & ((i == 0) | (be_ref[i] != be_ref[jnp.maximum(i - 1, 0)])))
    def _():
        w1b[...] = w1_ref[0, 0].astype(BF16)
        w3b[...] = w3_ref[0, 0].astype(BF16)
        w2b[...] = w2_ref[0, 0].astype(BF16)

    @pl.when(n_cur > 0)
    def _():
        wait_gather(slot)
        w = xbuf[slot]
        dh = w.shape[1]
        xl = lax.bitcast_convert_type(w & jnp.uint32(0xFFFF0000), F32).astype(BF16)
        xr = lax.bitcast_convert_type(w << 16, F32).astype(BF16)
        a1 = (jnp.dot(xl, w1b[0:dh, :], preferred_element_type=F32)
              + jnp.dot(xr, w1b[dh:, :], preferred_element_type=F32))
        a3 = (jnp.dot(xl, w3b[0:dh, :], preferred_element_type=F32)
              + jnp.dot(xr, w3b[dh:, :], preferred_element_type=F32))
        hid = (a1 * _sigmoid(a1) * a3 * g_ref[...]).astype(BF16)
        ybuf[slot] = jnp.dot(hid, w2b[...], preferred_element_type=F32)
        issue_scatter(i, slot, n_cur)

    @pl.when(i >= 1)
    def _():
        n_prev = nr_ref[jnp.maximum(i - 1, 0)]

        @pl.when(n_prev > 0)
        def _():
            wait_scatter(1 - slot, n_prev)

    @pl.when((i == n_steps - 1) & (n_cur > 0))
    def _():
        wait_scatter(slot, n_cur)


def _expert_mlp(hpack, idx, blk_expert, nreal, gate_buf, w1, w3, w2, layer, z_rows):
    t, dh = hpack.shape
    d = 2 * dh
    p = idx.shape[0]
    n_steps = p // MOE_BLK
    de = w1.shape[3]
    assert MOE_BLK & (MOE_BLK - 1) == 0
    return pl.pallas_call(
        functools.partial(_expert_kernel, blk=MOE_BLK, n_steps=n_steps),
        grid_spec=pltpu.PrefetchScalarGridSpec(
            num_scalar_prefetch=3,
            grid=(n_steps,),
            in_specs=[
                pl.BlockSpec(memory_space=pl.ANY),
                pl.BlockSpec((1, 1, d, de), lambda i, ix, be, nr: (layer, be[i], 0, 0)),
                pl.BlockSpec((1, 1, d, de), lambda i, ix, be, nr: (layer, be[i], 0, 0)),
                pl.BlockSpec((1, 1, de, d), lambda i, ix, be, nr: (layer, be[i], 0, 0)),
                pl.BlockSpec((MOE_BLK, 1), lambda i, ix, be, nr: (i, 0)),
            ],
            out_specs=pl.BlockSpec(memory_space=pl.ANY),
            scratch_shapes=[
                pltpu.VMEM((2, MOE_BLK, dh), jnp.uint32),
                pltpu.VMEM((2, MOE_BLK, d), F32),
                pltpu.VMEM((d, de), BF16),
                pltpu.VMEM((d, de), BF16),
                pltpu.VMEM((de, d), BF16),
                pltpu.SemaphoreType.DMA((2,)),
                pltpu.SemaphoreType.DMA((2,)),
            ],
        ),
        out_shape=jax.ShapeDtypeStruct((z_rows, d), F32),
        compiler_params=_cparams(("arbitrary",)),
        name="expert_mlp",
    )(idx, blk_expert, nreal, hpack, w1, w3, w2, gate_buf)


def _combine_kernel(x_ref, z_ref, o_ref):
    acc = x_ref[...]
    for k in range(z_ref.shape[0]):
        acc = acc + z_ref[k]
    o_ref[...] = acc


def _combine(x, z3):
    t, d = x.shape
    tm = min(COMB_TM, t)
    assert t % tm == 0
    return pl.pallas_call(
        _combine_kernel,
        grid=(t // tm,),
        in_specs=[
            pl.BlockSpec((tm, d), lambda i: (i, 0)),
            pl.BlockSpec((z3.shape[0], tm, d), lambda i: (0, i, 0)),
        ],
        out_specs=pl.BlockSpec((tm, d), lambda i: (i, 0)),
        out_shape=jax.ShapeDtypeStruct((t, d), F32),
        input_output_aliases={0: 0},
        compiler_params=_cparams(("parallel",)),
        name="moe_combine",
    )(x, z3)


def _t5_causal_bucket(dist):
    max_exact = N_BUCKETS // 2
    d_f = jnp.maximum(dist, 1).astype(F32)
    large = max_exact + (jnp.log(d_f / max_exact) / math.log(MAX_DISTANCE / max_exact)
                         * (N_BUCKETS - max_exact)).astype(jnp.int32)
    large = jnp.minimum(large, N_BUCKETS - 1)
    return jnp.where(dist < max_exact, dist, large)


def _bias_tiles(rel_bias_table, s, t):
    assert t >= MAX_DISTANCE
    rbd = rel_bias_table[_t5_causal_bucket(jnp.arange(s))].T.astype(F32)
    far = rel_bias_table[N_BUCKETS - 1].astype(F32)
    rbd = rbd - far[:, None]
    h = rbd.shape[0]

    def toeplitz(w):
        a = jnp.broadcast_to(w[:, None, :], (h, t, 2 * t)).reshape(h, 2 * t * t)
        return a[:, :t * (2 * t - 1)].reshape(h, t, 2 * t - 1)[:, :, :t]

    k = np.arange(2 * t)
    d_diag = np.where(k == 0, 0, np.minimum(2 * t - k, s - 1))
    w_diag = jnp.where((k >= 1) & (k <= t), NEG_INF, rbd[:, d_diag])
    d_off = np.minimum(np.where(k < t, t - k, 3 * t - k), s - 1)
    w_off = rbd[:, d_off]
    off = toeplitz(w_off).transpose(0, 2, 1) * LOG2E
    diag = jnp.maximum(toeplitz(w_diag).transpose(0, 2, 1) * LOG2E, NEG_INF)
    return jnp.concatenate([off, off], axis=2), jnp.concatenate([diag, diag], axis=2)


def _moe_plan(rt, t, n_experts, blk):
    nblk_rt, _, tq = rt.shape
    ids = rt[:, 0:TOP_K, :].astype(jnp.int32).transpose(0, 2, 1).reshape(t * TOP_K)
    gates = rt[:, TOP_K:2 * TOP_K, :].transpose(0, 2, 1).reshape(t * TOP_K)
    a = t * TOP_K
    order = jnp.argsort(ids).astype(jnp.int32)
    counts = jnp.sum((ids[None, :] == jnp.arange(n_experts)[:, None]).astype(jnp.int32), axis=1)
    start = jnp.cumsum(counts) - counts
    pcounts = (counts + blk - 1) // blk * blk
    pend = jnp.cumsum(pcounts)
    pstart = pend - pcounts
    n_blocks = -(-(a + n_experts * (blk - 1)) // blk)
    p = n_blocks * blk
    blk_row0 = jnp.arange(n_blocks, dtype=jnp.int32) * blk
    blk_expert = jnp.minimum(jnp.sum((pend[None, :] <= blk_row0[:, None]).astype(jnp.int32), axis=1),
                             n_experts - 1)
    nreal = jnp.clip((pstart + counts)[blk_expert] - blk_row0, 0, blk).astype(jnp.int32)
    within = (blk_row0 - pstart[blk_expert])[:, None] + jnp.arange(blk, dtype=jnp.int32)[None, :]
    valid = jnp.arange(blk, dtype=jnp.int32)[None, :] < nreal[:, None]
    src = jnp.where(valid, start[blk_expert][:, None] + within, 0).reshape(p)
    valid = valid.reshape(p)
    asg = order[src].astype(jnp.uint32)
    zrow = (asg % TOP_K) * t + asg // TOP_K
    idx = jnp.where(valid, (zrow << TOK_BITS) | (asg // TOP_K), 0).astype(jnp.uint32)
    idx = lax.bitcast_convert_type(idx, jnp.int32)
    gate_buf = jnp.where(valid, gates[order[src]], 0.0)
    nreal = jnp.concatenate([nreal, jnp.zeros((1,), jnp.int32)])
    return idx, blk_expert.astype(jnp.int32), nreal, gate_buf.reshape(p, 1)


def kernel(x, mem, rel_bias_table, g_mix, w_in, g_q, g_k, diff_lambda, g_subln, conv_w, conv_b,
           conv_ln_g, conv_ln_b, w_out, g_cross, g_mem, wq_c, wkv_c, g_qc, g_kc, wo_c, g_ffn,
           w_group, b_group, w_router, b_router, w1, w3, w2):
    b, s, d = x.shape
    mlen = mem.shape[1]
    depth = w_in.shape[0]
    t = b * s
    d_conv = conv_w.shape[2]
    n_heads = (w_out.shape[1] - d_conv) // DV
    d_qk = n_heads * 2 * DQ
    n_experts = w_router.shape[2]
    assert t <= (1 << TOK_BITS) and TOP_K * t <= (1 << (32 - TOK_BITS))

    boff, bdiag = _bias_tiles(rel_bias_table, s, min(ATTN_T, s))
    n_router_rows = -(-(N_GROUPS + n_experts) // LANES) * LANES

    for l in range(depth):
        lam_init = 0.8 - 0.6 * math.exp(-0.3 * l)
        dl = diff_lambda[l].astype(F32)
        lam = jnp.exp(jnp.sum(dl[0] * dl[1])) - jnp.exp(jnp.sum(dl[2] * dl[3])) + lam_init
        gs = g_subln[l] * (1.0 - lam_init)
        qk_gain = jnp.concatenate([
            jnp.tile(g_q[l] * (DQ ** -0.5 * LOG2E), d_qk // DQ), jnp.tile(g_k[l], d_qk // DQ),
            jnp.ones((w_in.shape[2] - 2 * d_qk,), F32)]).reshape(1, -1)

        proj = _norm_matmul(x.reshape(t, d), g_mix[l], w_in[l].astype(BF16), qk_gain, 2 * d_qk)
        proj = proj.reshape(b, s, -1)
        attn = _diff_attention(proj, boff, bdiag, lam, gs, n_heads)
        conv = _conformer_conv(proj, 2 * d_qk + n_heads * DV, conv_w[l], conv_b[l],
                               conv_ln_g[l], conv_ln_b[l])
        x = _outproj_residual(x.reshape(t, d), attn.reshape(t, -1), conv.reshape(t, -1),
                              w_out[l].astype(BF16), in_place=l > 0).reshape(b, s, d)

        kv = _norm_matmul(mem.reshape(b * mlen, d), g_mem[l], wkv_c[l].astype(BF16))
        wr = jnp.zeros((n_router_rows, d), F32)
        wr = wr.at[0:N_GROUPS].set(w_group[l].T).at[N_GROUPS:N_GROUPS + n_experts].set(w_router[l].T)
        wr_hi = wr.astype(BF16)
        wr_lo = (wr - wr_hi.astype(F32)).astype(BF16)
        b_r = jnp.zeros((n_router_rows, 1), F32)
        b_r = b_r.at[0:N_GROUPS, 0].set(b_group[l]).at[N_GROUPS:N_GROUPS + n_experts, 0].set(b_router[l])
        x, hpack, rt = _cross_block(x, kv.reshape(b, mlen, -1), g_cross[l], wq_c[l].astype(BF16),
                                    g_qc[l], g_kc[l], wo_c[l].astype(BF16), g_ffn[l], wr_hi, wr_lo, b_r)

        idx, blk_expert, nreal, gate_buf = _moe_plan(rt, t, n_experts, MOE_BLK)
        z = _expert_mlp(hpack.reshape(t, d // 2), idx, blk_expert, nreal, gate_buf,
                        w1, w3, w2, l, TOP_K * t)
        x = _combine(x.reshape(t, d), z.reshape(TOP_K, t, d)).reshape(b, s, d)
    return x
```

```python
import functools
import math

import jax
import jax.numpy as jnp
import numpy as np
from jax import lax
from jax.experimental import pallas as pl
from jax.experimental.pallas import tpu as pltpu

DQ = 64
DV = 2 * DQ
CONV_WIDTH = 31
N_BUCKETS = 32
MAX_DISTANCE = 128
N_HEADS_C = 4
DH_C = 128
N_GROUPS = 8
EXPERTS_PER_GROUP = 8
TOP_K = 2
EPS = 1e-6
NEG_INF = -1e30
LOG2E = 1.4426950408889634

LANES = 128
MXU_DIM = 256
VMEM_LIMIT_BYTES = 56 * 1024 * 1024

F32 = jnp.float32
BF16 = jnp.bfloat16

PROJ_TM = 1024
PROJ_TN = 512
ATTN_T = 256
CONV_TS = 256
CONV_HALO = 32
CROSS_TQ = 256
MOE_BLK = 256
COMB_TM = 256
TOK_BITS = 15


def _cparams(sem):
    return pltpu.CompilerParams(dimension_semantics=sem, vmem_limit_bytes=VMEM_LIMIT_BYTES)


def _sigmoid(x):
    return 1.0 / (1.0 + jnp.exp(-x))


def _norm_matmul_kernel(x_ref, g_ref, w_ref, qkg_ref, seg_ref, o_ref, xn_ref, *, n_qk_tiles):
    j = pl.program_id(1)

    @pl.when(j == 0)
    def _():
        x = x_ref[...]
        inv = lax.rsqrt(jnp.mean(x * x, axis=-1, keepdims=True) + EPS)
        xn_ref[...] = (x * inv * g_ref[...]).astype(BF16)

    acc = jnp.dot(xn_ref[...], w_ref[...], preferred_element_type=F32)

    if n_qk_tiles == 0:
        o_ref[...] = acc.astype(o_ref.dtype)
        return

    @pl.when(j < n_qk_tiles)
    def _():
        sq = (acc * acc).astype(BF16)
        seg = seg_ref[...]
        parts = [jnp.dot(sq[:, c * MXU_DIM:(c + 1) * MXU_DIM], seg, preferred_element_type=F32)
                 for c in range(acc.shape[1] // MXU_DIM)]
        ss = jnp.concatenate(parts, axis=1)
        o_ref[...] = (acc * lax.rsqrt(ss * (1.0 / DQ) + EPS) * qkg_ref[...]).astype(o_ref.dtype)

    @pl.when(j >= n_qk_tiles)
    def _():
        o_ref[...] = acc.astype(o_ref.dtype)


def _norm_matmul(x, g, w, qk_gain=None, n_qk_cols=0):
    m, d = x.shape
    n = w.shape[1]
    tm = min(PROJ_TM, m)
    tn = min(PROJ_TN, n)
    assert m % tm == 0 and n % tn == 0 and n_qk_cols % tn == 0 and tn % MXU_DIM == 0
    if qk_gain is None:
        qk_gain = jnp.ones((1, n), F32)
    r = jnp.arange(MXU_DIM) // DQ
    seg = (r[:, None] == r[None, :]).astype(BF16)
    return pl.pallas_call(
        functools.partial(_norm_matmul_kernel, n_qk_tiles=n_qk_cols // tn),
        grid=(m // tm, n // tn),
        in_specs=[
            pl.BlockSpec((tm, d), lambda i, j: (i, 0)),
            pl.BlockSpec((1, d), lambda i, j: (0, 0)),
            pl.BlockSpec((d, tn), lambda i, j: (0, j)),
            pl.BlockSpec((1, tn), lambda i, j: (0, j)),
            pl.BlockSpec((MXU_DIM, MXU_DIM), lambda i, j: (0, 0)),
        ],
        out_specs=pl.BlockSpec((tm, tn), lambda i, j: (i, j)),
        out_shape=jax.ShapeDtypeStruct((m, n), BF16),
        scratch_shapes=[pltpu.VMEM((tm, d), BF16)],
        compiler_params=_cparams(("parallel", "arbitrary")),
        name="norm_matmul",
    )(x, g.reshape(1, d), w, qk_gain, seg)


def _attn_kernel(q_ref, k_ref, v_ref, boff_ref, bdiag_ref, lam_ref, gs_ref, o_ref,
                 qs_ref, vt_ref, m_ref, l_ref, acc_ref, *, t):
    i = pl.program_id(2)
    s_len = k_ref.shape[1]

    @pl.when(i == 0)
    def _():
        for c in range(s_len // t):
            vt_ref[c] = v_ref[0, c * t:(c + 1) * t, :].astype(F32).T.astype(BF16)

    q = q_ref[0].astype(F32)
    lane = lax.broadcasted_iota(jnp.int32, q.shape, 1)
    qs_ref[0:t, :] = jnp.where(lane < DQ, q, 0.0).astype(BF16)
    qs_ref[t:2 * t, :] = jnp.where(lane >= DQ, q, 0.0).astype(BF16)
    m_ref[...] = jnp.full(m_ref.shape, NEG_INF, F32)
    l_ref[...] = jnp.zeros(l_ref.shape, F32)
    acc_ref[...] = jnp.zeros(acc_ref.shape, F32)

    def scores(j):
        off = pl.multiple_of(j * t, t)
        k = k_ref[0, pl.ds(off, t), :]
        return lax.dot_general(k, qs_ref[...], (((1,), (1,)), ((), ())), preferred_element_type=F32)

    def softmax_pv(j, st):
        m_prev = m_ref[...]
        m_new = jnp.maximum(m_prev, jnp.max(st, axis=0, keepdims=True))
        alpha = jnp.exp2(m_prev - m_new)
        pt = jnp.exp2(st - m_new)
        l_ref[...] = alpha * l_ref[...] + jnp.sum(pt, axis=0, keepdims=True)
        pv = jnp.dot(vt_ref[j], pt.astype(BF16), preferred_element_type=F32)
        acc_ref[...] = alpha * acc_ref[...] + pv
        m_ref[...] = m_new

    def far_body(j, st):
        st_next = scores(j + 1)
        softmax_pv(j, st)
        return st_next

    def off_body(j, st):
        st_next = scores(j + 1)
        softmax_pv(j, st + boff_ref[0])
        return st_next

    st = lax.fori_loop(0, i - 1, far_body, scores(0))
    st = lax.fori_loop(jnp.maximum(i - 1, 0), i, off_body, st)
    softmax_pv(i, st + bdiag_ref[0])

    ot = acc_ref[...] / l_ref[...]
    ot = ot[:, 0:t] - lam_ref[...] * ot[:, t:2 * t]
    inv = lax.rsqrt(jnp.mean(ot * ot, axis=0, keepdims=True) + EPS)
    o_ref[0] = (ot * inv * gs_ref[...]).T.astype(o_ref.dtype)


def _diff_attention(proj, boff, bdiag, lam, gs, n_heads):
    b, s, _ = proj.shape
    t = boff.shape[1]
    assert s % t == 0
    lam_row = jnp.full((1, t), lam, F32)
    gs_col = jnp.broadcast_to(gs.astype(F32)[:, None], (DV, t))
    return pl.pallas_call(
        functools.partial(_attn_kernel, t=t),
        grid=(b, n_heads, s // t),
        in_specs=[
            pl.BlockSpec((1, t, DV), lambda bi, h, i: (bi, i, h)),
            pl.BlockSpec((1, s, DV), lambda bi, h, i: (bi, 0, n_heads + h)),
            pl.BlockSpec((1, s, DV), lambda bi, h, i: (bi, 0, 2 * n_heads + h)),
            pl.BlockSpec((1, t, 2 * t), lambda bi, h, i: (h, 0, 0)),
            pl.BlockSpec((1, t, 2 * t), lambda bi, h, i: (h, 0, 0)),
            pl.BlockSpec((1, t), lambda bi, h, i: (0, 0)),
            pl.BlockSpec((DV, t), lambda bi, h, i: (0, 0)),
        ],
        out_specs=pl.BlockSpec((1, t, DV), lambda bi, h, i: (bi, i, h)),
        out_shape=jax.ShapeDtypeStruct((b, s, n_heads * DV), BF16),
        scratch_shapes=[
            pltpu.VMEM((2 * t, DV), BF16),
            pltpu.VMEM((s // t, DV, t), BF16),
            pltpu.VMEM((1, 2 * t), F32),
            pltpu.VMEM((1, 2 * t), F32),
            pltpu.VMEM((DV, 2 * t), F32),
        ],
        compiler_params=_cparams(("parallel", "parallel", "arbitrary")),
        name="diff_attention",
    )(proj, proj, proj, boff, bdiag, lam_row, gs_col)


def _conv_kernel(a_ref, gate_ref, w_ref, b_ref, lg_ref, lb_ref, o_ref, uext_ref, y_ref, *, ts):
    si = pl.program_id(1)

    @pl.when(si == 0)
    def _():
        uext_ref[0:CONV_HALO, :] = jnp.zeros((CONV_HALO, uext_ref.shape[1]), F32)

    @pl.when(si > 0)
    def _():
        uext_ref[0:CONV_HALO, :] = uext_ref[ts:ts + CONV_HALO, :]

    a = a_ref[0].astype(F32)
    gate = gate_ref[0].astype(F32)
    uext_ref[CONV_HALO:CONV_HALO + ts, :] = a * _sigmoid(gate)

    base = CONV_HALO - (CONV_WIDTH - 1)
    for c in range(uext_ref.shape[1] // LANES):
        cs = slice(c * LANES, (c + 1) * LANES)
        acc = jnp.broadcast_to(b_ref[:, cs], (ts, LANES))
        for j in range(CONV_WIDTH):
            acc = acc + w_ref[j:j + 1, cs] * uext_ref[base + j:base + j + ts, cs]
        y_ref[:, cs] = acc

    y = y_ref[...]
    mu = jnp.mean(y, axis=-1, keepdims=True)
    yc = y - mu
    var = jnp.mean(yc * yc, axis=-1, keepdims=True)
    z = yc * lax.rsqrt(var + EPS) * lg_ref[...] + lb_ref[...]
    o_ref[0] = (z * _sigmoid(z)).astype(o_ref.dtype)


def _conformer_conv(proj, col0, conv_w, conv_b, ln_g, ln_b):
    b, s, _ = proj.shape
    c = conv_w.shape[1]
    ts = min(CONV_TS, s)
    assert s % ts == 0 and col0 % c == 0 and ts >= CONV_HALO
    return pl.pallas_call(
        functools.partial(_conv_kernel, ts=ts),
        grid=(b, s // ts),
        in_specs=[
            pl.BlockSpec((1, ts, c), lambda bi, si: (bi, si, col0 // c)),
            pl.BlockSpec((1, ts, c), lambda bi, si: (bi, si, col0 // c + 1)),
            pl.BlockSpec((CONV_WIDTH, c), lambda bi, si: (0, 0)),
            pl.BlockSpec((1, c), lambda bi, si: (0, 0)),
            pl.BlockSpec((1, c), lambda bi, si: (0, 0)),
            pl.BlockSpec((1, c), lambda bi, si: (0, 0)),
        ],
        out_specs=pl.BlockSpec((1, ts, c), lambda bi, si: (bi, si, 0)),
        out_shape=jax.ShapeDtypeStruct((b, s, c), BF16),
        scratch_shapes=[pltpu.VMEM((CONV_HALO + ts, c), F32), pltpu.VMEM((ts, c), F32)],
        compiler_params=_cparams(("parallel", "arbitrary")),
        name="conformer_conv",
    )(proj, proj, conv_w, conv_b.reshape(1, c), ln_g.reshape(1, c), ln_b.reshape(1, c))


def _outproj_kernel(x_ref, a_ref, c_ref, w_ref, o_ref):
    ka = a_ref.shape[1]
    y = jnp.dot(a_ref[...], w_ref[0:ka, :], preferred_element_type=F32)
    y = y + jnp.dot(c_ref[...], w_ref[ka:, :], preferred_element_type=F32)
    o_ref[...] = x_ref[...] + y


def _outproj_residual(x, attn, conv, w, in_place):
    m, d = x.shape
    ka, kc = attn.shape[1], conv.shape[1]
    tm = min(PROJ_TM, m)
    tn = min(2 * PROJ_TN, d)
    assert m % tm == 0 and d % tn == 0
    return pl.pallas_call(
        _outproj_kernel,
        grid=(m // tm, d // tn),
        in_specs=[
            pl.BlockSpec((tm, tn), lambda i, j: (i, j)),
            pl.BlockSpec((tm, ka), lambda i, j: (i, 0)),
            pl.BlockSpec((tm, kc), lambda i, j: (i, 0)),
            pl.BlockSpec((ka + kc, tn), lambda i, j: (0, j)),
        ],
        out_specs=pl.BlockSpec((tm, tn), lambda i, j: (i, j)),
        out_shape=jax.ShapeDtypeStruct((m, d), F32),
        input_output_aliases={0: 0} if in_place else {},
        compiler_params=_cparams(("parallel", "parallel")),
        name="outproj_residual",
    )(x, attn, conv, w)


def _cross_kernel(x_ref, gc_ref, wq_ref, kv_ref, gq_ref, gk_ref, wo_ref, gf_ref,
                  wrh_ref, wrl_ref, br_ref, xo_ref, hp_ref, rt_ref):
    x = x_ref[0]
    d = x.shape[1]
    tq = x.shape[0]
    h = (x * lax.rsqrt(jnp.mean(x * x, axis=-1, keepdims=True) + EPS) * gc_ref[...]).astype(BF16)
    q = jnp.dot(h, wq_ref[...], preferred_element_type=F32)
    dc = N_HEADS_C * DH_C
    outs = []
    for hd in range(N_HEADS_C):
        cs = slice(hd * DH_C, (hd + 1) * DH_C)
        qh = q[:, cs]
        qh = qh * lax.rsqrt(jnp.mean(qh * qh, axis=-1, keepdims=True) + EPS) * gq_ref[...]
        kh = kv_ref[0, :, cs].astype(F32)
        kh = kh * lax.rsqrt(jnp.mean(kh * kh, axis=-1, keepdims=True) + EPS) * gk_ref[...]
        vh = kv_ref[0, :, dc + hd * DH_C:dc + (hd + 1) * DH_C]
        s = lax.dot_general(qh.astype(BF16), kh.astype(BF16), (((1,), (1,)), ((), ())),
                            preferred_element_type=F32)
        p = jnp.exp(s - jnp.max(s, axis=-1, keepdims=True))
        p = p / jnp.sum(p, axis=-1, keepdims=True)
        outs.append(jnp.dot(p.astype(BF16), vh, preferred_element_type=F32))
    o = jnp.concatenate(outs, axis=1).astype(BF16)
    x2 = x + jnp.dot(o, wo_ref[...], preferred_element_type=F32)
    xo_ref[0] = x2

    h2 = x2 * lax.rsqrt(jnp.mean(x2 * x2, axis=-1, keepdims=True) + EPS) * gf_ref[...]
    h2_hi = h2.astype(BF16)
    h2_hi32 = h2_hi.astype(F32)
    h2_lo = (h2 - h2_hi32).astype(BF16)

    bits = lax.bitcast_convert_type(h2_hi32, jnp.uint32)
    hp_ref[0] = bits[:, 0:d // 2] | (bits[:, d // 2:] >> 16)

    nt = (((1,), (1,)), ((), ()))
    lt = (lax.dot_general(wrh_ref[...], h2_hi, nt, preferred_element_type=F32)
          + lax.dot_general(wrh_ref[...], h2_lo, nt, preferred_element_type=F32)
          + lax.dot_general(wrl_ref[...], h2_hi, nt, preferred_element_type=F32)) + br_ref[...]
    g, e = N_GROUPS, EXPERTS_PER_GROUP
    gl = lt[0:g]
    rowg = lax.broadcasted_iota(jnp.int32, gl.shape, 0)
    gmax = jnp.max(gl, axis=0, keepdims=True)
    gsel = jnp.min(jnp.where(gl == gmax, rowg, g), axis=0, keepdims=True)
    gw = 1.0 / jnp.sum(jnp.exp(gl - gmax), axis=0, keepdims=True)
    el = jnp.zeros((e, tq), F32)
    for gi in range(g):
        el = jnp.where(gsel == gi, lt[g + gi * e:g + (gi + 1) * e], el)
    rowe = lax.broadcasted_iota(jnp.int32, el.shape, 0)
    v1 = jnp.max(el, axis=0, keepdims=True)
    i1 = jnp.min(jnp.where(el == v1, rowe, e), axis=0, keepdims=True)
    el2 = jnp.where(rowe == i1, -jnp.inf, el)
    v2 = jnp.max(el2, axis=0, keepdims=True)
    i2 = jnp.min(jnp.where(el2 == v2, rowe, e), axis=0, keepdims=True)
    e2 = jnp.exp(v2 - v1)
    den = 1.0 / (1.0 + e2)
    id1 = (gsel * e + i1).astype(F32)
    id2 = (gsel * e + i2).astype(F32)
    row8 = lax.broadcasted_iota(jnp.int32, (8, tq), 0)
    rt = jnp.where(row8 == 0, id1, jnp.where(row8 == 1, id2,
         jnp.where(row8 == 2, gw * den, jnp.where(row8 == 3, gw * e2 * den, 0.0))))
    rt_ref[0] = rt


def _cross_block(x, kv, g_cross, wq, g_qc, g_kc, wo, g_ffn, wr_hi, wr_lo, b_r):
    b, s, d = x.shape
    mlen = kv.shape[1]
    tq = min(CROSS_TQ, s)
    nq = s // tq
    dc = N_HEADS_C * DH_C
    nr = wr_hi.shape[0]
    assert s % tq == 0
    const = lambda bi, qi: (0, 0)
    return pl.pallas_call(
        _cross_kernel,
        grid=(b, nq),
        in_specs=[
            pl.BlockSpec((1, tq, d), lambda bi, qi: (bi, qi, 0)),
            pl.BlockSpec((1, d), const),
            pl.BlockSpec((d, dc), const),
            pl.BlockSpec((1, mlen, 2 * dc), lambda bi, qi: (bi, 0, 0)),
            pl.BlockSpec((1, DH_C), const),
            pl.BlockSpec((1, DH_C), const),
            pl.BlockSpec((dc, d), const),
            pl.BlockSpec((1, d), const),
            pl.BlockSpec((nr, d), const),
            pl.BlockSpec((nr, d), const),
            pl.BlockSpec((nr, 1), const),
        ],
        out_specs=[
            pl.BlockSpec((1, tq, d), lambda bi, qi: (bi, qi, 0)),
            pl.BlockSpec((1, tq, d // 2), lambda bi, qi: (bi, qi, 0)),
            pl.BlockSpec((1, 8, tq), lambda bi, qi: (bi * nq + qi, 0, 0)),
        ],
        out_shape=[
            jax.ShapeDtypeStruct((b, s, d), F32),
            jax.ShapeDtypeStruct((b, s, d // 2), jnp.uint32),
            jax.ShapeDtypeStruct((b * nq, 8, tq), F32),
        ],
        input_output_aliases={0: 0},
        compiler_params=_cparams(("parallel", "parallel")),
        name="cross_block",
    )(x, g_cross.reshape(1, d), wq, kv, (g_qc * DH_C ** -0.5).reshape(1, DH_C), g_kc.reshape(1, DH_C),
      wo, g_ffn.reshape(1, d), wr_hi, wr_lo, b_r)


def _expert_kernel(idx_ref, be_ref, nr_ref, h_hbm, w1_ref, w3_ref, w2_ref, g_ref, z_hbm,
                   xbuf, ybuf, w1b, w3b, w2b, gsem, ssem, *, blk, n_steps):
    i = pl.program_id(0)
    n_cur = nr_ref[i]
    i_prev = jnp.maximum(i - 1, 0)
    n_prev = jnp.where(i >= 1, nr_ref[i_prev], 0)
    slot = i % 2
    tok_mask = (1 << TOK_BITS) - 1

    def issue_gather(step, sl):
        def body(r, carry):
            row = idx_ref[step * blk + r] & tok_mask
            pltpu.make_async_copy(h_hbm.at[pl.ds(row, 1)], xbuf.at[sl, pl.ds(r, 1)], gsem.at[sl]).start()
            return carry
        lax.fori_loop(0, blk, body, 0, unroll=8)

    def wait_gather(sl):
        pltpu.make_async_copy(h_hbm.at[pl.ds(0, blk)], xbuf.at[sl], gsem.at[sl]).wait()

    def issue_scatter(step, sl, n):
        def body(r, carry):
            row = lax.shift_right_logical(idx_ref[step * blk + r], TOK_BITS)

            @pl.when(r < n)
            def _():
                pltpu.make_async_copy(ybuf.at[sl, pl.ds(r, 1)], z_hbm.at[pl.ds(row, 1)], ssem.at[sl]).start()
            return carry
        lax.fori_loop(0, blk, body, 0, unroll=8)

    def wait_scatter(sl, n):
        k = blk
        while k >= 1:
            @pl.when((n & k) != 0)
            def _(k=k):
                pltpu.make_async_copy(ybuf.at[sl, pl.ds(0, k)], z_hbm.at[pl.ds(0, k)], ssem.at[sl]).wait()
            k //= 2

    @pl.when(i == 0)
    def _():
        issue_gather(0, 0)

    @pl.when((n_cur > 0) & ((i == 0) | (be_ref[i] != be_ref[jnp.maximum(i - 1, 0)])))
    def _():
        w1b[...] = w1_ref[0, 0].astype(BF16)
        w3b[...] = w3_ref[0, 0].astype(BF16)
        w2b[...] = w2_ref[0, 0].astype(BF16)

    def block_body(sl):
        wait_gather(sl)
        w = xbuf[sl]
        dh = w.shape[1]
        xl = lax.bitcast_convert_type(w & jnp.uint32(0xFFFF0000), F32).astype(BF16)
        xr = lax.bitcast_convert_type(w << 16, F32).astype(BF16)
        for r in range(blk):
            row = idx_ref[(i + 1) * blk + r] & tok_mask
            pltpu.make_async_copy(h_hbm.at[pl.ds(row, 1)], xbuf.at[1 - sl, pl.ds(r, 1)],
                                  gsem.at[1 - sl]).start()
        for r in range(blk):
            row = lax.shift_right_logical(idx_ref[i_prev * blk + r], TOK_BITS)

            @pl.when(r < n_prev)
            def _(r=r, row=row):
                pltpu.make_async_copy(ybuf.at[1 - sl, pl.ds(r, 1)], z_hbm.at[pl.ds(row, 1)],
                                      ssem.at[1 - sl]).start()
        a1 = (jnp.dot(xl, w1b[0:dh, :], preferred_element_type=F32)
              + jnp.dot(xr, w1b[dh:, :], preferred_element_type=F32))
        a3 = (jnp.dot(xl, w3b[0:dh, :], preferred_element_type=F32)
              + jnp.dot(xr, w3b[dh:, :], preferred_element_type=F32))
        hid = (a1 * _sigmoid(a1) * a3 * g_ref[...]).astype(BF16)
        ybuf[sl] = jnp.dot(hid, w2b[...], preferred_element_type=F32)
        wait_scatter(1 - sl, n_prev)

        @pl.when(i == n_steps - 1)
        def _():
            wait_gather(1 - sl)
            issue_scatter(i, sl, n_cur)
            wait_scatter(sl, n_cur)

    for sl in range(2):
        @pl.when((n_cur > 0) & (slot == sl))
        def _(sl=sl):
            block_body(sl)

    @pl.when((n_cur == 0) & (n_prev > 0))
    def _():
        wait_gather(slot)
        issue_scatter(i_prev, 1 - slot, n_prev)
        wait_scatter(1 - slot, n_prev)


def _expert_mlp(hpack, idx, blk_expert, nreal, gate_buf, w1, w3, w2, layer, z_rows):
    t, dh = hpack.shape
    d = 2 * dh
    n_steps = idx.shape[0] // MOE_BLK - 1
    de = w1.shape[3]
    assert MOE_BLK & (MOE_BLK - 1) == 0
    return pl.pallas_call(
        functools.partial(_expert_kernel, blk=MOE_BLK, n_steps=n_steps),
        grid_spec=pltpu.PrefetchScalarGridSpec(
            num_scalar_prefetch=3,
            grid=(n_steps,),
            in_specs=[
                pl.BlockSpec(memory_space=pl.ANY),
                pl.BlockSpec((1, 1, d, de), lambda i, ix, be, nr: (layer, be[i], 0, 0)),
                pl.BlockSpec((1, 1, d, de), lambda i, ix, be, nr: (layer, be[i], 0, 0)),
                pl.BlockSpec((1, 1, de, d), lambda i, ix, be, nr: (layer, be[i], 0, 0)),
                pl.BlockSpec((MOE_BLK, 1), lambda i, ix, be, nr: (i, 0)),
            ],
            out_specs=pl.BlockSpec(memory_space=pl.ANY),
            scratch_shapes=[
                pltpu.VMEM((2, MOE_BLK, dh), jnp.uint32),
                pltpu.VMEM((2, MOE_BLK, d), F32),
                pltpu.VMEM((d, de), BF16),
                pltpu.VMEM((d, de), BF16),
                pltpu.VMEM((de, d), BF16),
                pltpu.SemaphoreType.DMA((2,)),
                pltpu.SemaphoreType.DMA((2,)),
            ],
        ),
        out_shape=jax.ShapeDtypeStruct((z_rows, d), F32),
        compiler_params=_cparams(("arbitrary",)),
        name="expert_mlp",
    )(idx, blk_expert, nreal, hpack, w1, w3, w2, gate_buf)


def _combine_kernel(x_ref, z_ref, o_ref):
    acc = x_ref[...]
    for k in range(z_ref.shape[0]):
        acc = acc + z_ref[k]
    o_ref[...] = acc


def _combine(x, z3):
    t, d = x.shape
    tm = min(COMB_TM, t)
    assert t % tm == 0
    return pl.pallas_call(
        _combine_kernel,
        grid=(t // tm,),
        in_specs=[
            pl.BlockSpec((tm, d), lambda i: (i, 0)),
            pl.BlockSpec((z3.shape[0], tm, d), lambda i: (0, i, 0)),
        ],
        out_specs=pl.BlockSpec((tm, d), lambda i: (i, 0)),
        out_shape=jax.ShapeDtypeStruct((t, d), F32),
        input_output_aliases={0: 0},
        compiler_params=_cparams(("parallel",)),
        name="moe_combine",
    )(x, z3)


def _t5_causal_bucket(dist):
    max_exact = N_BUCKETS // 2
    d_f = jnp.maximum(dist, 1).astype(F32)
    large = max_exact + (jnp.log(d_f / max_exact) / math.log(MAX_DISTANCE / max_exact)
                         * (N_BUCKETS - max_exact)).astype(jnp.int32)
    large = jnp.minimum(large, N_BUCKETS - 1)
    return jnp.where(dist < max_exact, dist, large)


def _bias_tiles(rel_bias_table, s, t):
    assert t >= MAX_DISTANCE
    rbd = rel_bias_table[_t5_causal_bucket(jnp.arange(s))].T.astype(F32)
    far = rel_bias_table[N_BUCKETS - 1].astype(F32)
    rbd = rbd - far[:, None]
    h = rbd.shape[0]

    def toeplitz(w):
        a = jnp.broadcast_to(w[:, None, :], (h, t, 2 * t)).reshape(h, 2 * t * t)
        return a[:, :t * (2 * t - 1)].reshape(h, t, 2 * t - 1)[:, :, :t]

    k = np.arange(2 * t)
    d_diag = np.where(k == 0, 0, np.minimum(2 * t - k, s - 1))
    w_diag = jnp.where((k >= 1) & (k <= t), NEG_INF, rbd[:, d_diag])
    d_off = np.minimum(np.where(k < t, t - k, 3 * t - k), s - 1)
    w_off = rbd[:, d_off]
    off = toeplitz(w_off).transpose(0, 2, 1) * LOG2E
    diag = jnp.maximum(toeplitz(w_diag).transpose(0, 2, 1) * LOG2E, NEG_INF)
    return jnp.concatenate([off, off], axis=2), jnp.concatenate([diag, diag], axis=2)


def _moe_plan(rt, t, n_experts, blk):
    nblk_rt, _, tq = rt.shape
    ids = rt[:, 0:TOP_K, :].astype(jnp.int32).transpose(0, 2, 1).reshape(t * TOP_K)
    gates = rt[:, TOP_K:2 * TOP_K, :].transpose(0, 2, 1).reshape(t * TOP_K)
    a = t * TOP_K
    order = jnp.argsort(ids).astype(jnp.int32)
    counts = jnp.sum((ids[None, :] == jnp.arange(n_experts)[:, None]).astype(jnp.int32), axis=1)
    start = jnp.cumsum(counts) - counts
    pcounts = (counts + blk - 1) // blk * blk
    pend = jnp.cumsum(pcounts)
    pstart = pend - pcounts
    n_blocks = -(-(a + n_experts * (blk - 1)) // blk)
    p = n_blocks * blk
    blk_row0 = jnp.arange(n_blocks, dtype=jnp.int32) * blk
    blk_expert = jnp.minimum(jnp.sum((pend[None, :] <= blk_row0[:, None]).astype(jnp.int32), axis=1),
                             n_experts - 1)
    nreal = jnp.clip((pstart + counts)[blk_expert] - blk_row0, 0, blk).astype(jnp.int32)
    within = (blk_row0 - pstart[blk_expert])[:, None] + jnp.arange(blk, dtype=jnp.int32)[None, :]
    valid = jnp.arange(blk, dtype=jnp.int32)[None, :] < nreal[:, None]
    src = jnp.where(valid, start[blk_expert][:, None] + within, 0).reshape(p)
    valid = valid.reshape(p)
    asg = order[src].astype(jnp.uint32)
    zrow = (asg % TOP_K) * t + asg // TOP_K
    idx = jnp.where(valid, (zrow << TOK_BITS) | (asg // TOP_K), 0).astype(jnp.uint32)
    idx = lax.bitcast_convert_type(idx, jnp.int32)
    idx = jnp.concatenate([idx, jnp.zeros((blk,), jnp.int32)])
    gate_buf = jnp.where(valid, gates[order[src]], 0.0)
    nreal = jnp.concatenate([nreal, jnp.zeros((1,), jnp.int32)])
    return idx, blk_expert.astype(jnp.int32), nreal, gate_buf.reshape(p, 1)


def kernel(x, mem, rel_bias_table, g_mix, w_in, g_q, g_k, diff_lambda, g_subln, conv_w, conv_b,
           conv_ln_g, conv_ln_b, w_out, g_cross, g_mem, wq_c, wkv_c, g_qc, g_kc, wo_c, g_ffn,
           w_group, b_group, w_router, b_router, w1, w3, w2):
    b, s, d = x.shape
    mlen = mem.shape[1]
    depth = w_in.shape[0]
    t = b * s
    d_conv = conv_w.shape[2]
    n_heads = (w_out.shape[1] - d_conv) // DV
    d_qk = n_heads * 2 * DQ
    n_experts = w_router.shape[2]
    assert t <= (1 << TOK_BITS) and TOP_K * t <= (1 << (32 - TOK_BITS))

    boff, bdiag = _bias_tiles(rel_bias_table, s, min(ATTN_T, s))
    n_router_rows = -(-(N_GROUPS + n_experts) // LANES) * LANES

    for l in range(depth):
        lam_init = 0.8 - 0.6 * math.exp(-0.3 * l)
        dl = diff_lambda[l].astype(F32)
        lam = jnp.exp(jnp.sum(dl[0] * dl[1])) - jnp.exp(jnp.sum(dl[2] * dl[3])) + lam_init
        gs = g_subln[l] * (1.0 - lam_init)
        qk_gain = jnp.concatenate([
            jnp.tile(g_q[l] * (DQ ** -0.5 * LOG2E), d_qk // DQ), jnp.tile(g_k[l], d_qk // DQ),
            jnp.ones((w_in.shape[2] - 2 * d_qk,), F32)]).reshape(1, -1)

        proj = _norm_matmul(x.reshape(t, d), g_mix[l], w_in[l].astype(BF16), qk_gain, 2 * d_qk)
        proj = proj.reshape(b, s, -1)
        attn = _diff_attention(proj, boff, bdiag, lam, gs, n_heads)
        conv = _conformer_conv(proj, 2 * d_qk + n_heads * DV, conv_w[l], conv_b[l],
                               conv_ln_g[l], conv_ln_b[l])
        x = _outproj_residual(x.reshape(t, d), attn.reshape(t, -1), conv.reshape(t, -1),
                              w_out[l].astype(BF16), in_place=l > 0).reshape(b, s, d)

        kv = _norm_matmul(mem.reshape(b * mlen, d), g_mem[l], wkv_c[l].astype(BF16))
        wr = jnp.zeros((n_router_rows, d), F32)
        wr = wr.at[0:N_GROUPS].set(w_group[l].T).at[N_GROUPS:N_GROUPS + n_experts].set(w_router[l].T)
        wr_hi = wr.astype(BF16)
        wr_lo = (wr - wr_hi.astype(F32)).astype(BF16)
        b_r = jnp.zeros((n_router_rows, 1), F32)
        b_r = b_r.at[0:N_GROUPS, 0].set(b_group[l]).at[N_GROUPS:N_GROUPS + n_experts, 0].set(b_router[l])
        x, hpack, rt = _cross_block(x, kv.reshape(b, mlen, -1), g_cross[l], wq_c[l].astype(BF16),
                                    g_qc[l], g_kc[l], wo_c[l].astype(BF16), g_ffn[l], wr_hi, wr_lo, b_r)

        idx, blk_expert, nreal, gate_buf = _moe_plan(rt, t, n_experts, MOE_BLK)
        z = _expert_mlp(hpack.reshape(t, d // 2), idx, blk_expert, nreal, gate_buf,
                        w1, w3, w2, l, TOP_K * t)
        x = _combine(x.reshape(t, d), z.reshape(TOP_K, t, d)).reshape(b, s, d)
    return x
```

```python
import functools
import math

import jax
import jax.numpy as jnp
import numpy as np
from jax import lax
from jax.experimental import pallas as pl
from jax.experimental.pallas import tpu as pltpu

DQ = 64
DV = 2 * DQ
CONV_WIDTH = 31
N_BUCKETS = 32
MAX_DISTANCE = 128
N_HEADS_C = 4
DH_C = 128
N_GROUPS = 8
EXPERTS_PER_GROUP = 8
TOP_K = 2
EPS = 1e-6
NEG_INF = -1e30
LOG2E = 1.4426950408889634

LANES = 128
SUBLANES = 8
MXU_DIM = 256
VMEM_LIMIT_BYTES = 56 * 1024 * 1024

F32 = jnp.float32
BF16 = jnp.bfloat16

PROJ_TM = 1024
PROJ_TN = 512
ATTN_T = 256
CONV_TS = 256
CONV_HALO = 32
CROSS_TQ = 256
MOE_BLK = 256
COMB_TM = 256
TOK_BITS = 15


def _cparams(sem):
    return pltpu.CompilerParams(dimension_semantics=sem, vmem_limit_bytes=VMEM_LIMIT_BYTES)


def _sigmoid(x):
    return 1.0 / (1.0 + jnp.exp(-x))


def _store_token_major(ref, val):
    n = val.shape[0]
    for c in range(SUBLANES):
        ref[pl.ds(c, n, stride=SUBLANES), :] = val[:, c * LANES:(c + 1) * LANES]


def _load_token_major(ref, n):
    return jnp.concatenate([ref[pl.ds(c, n, stride=SUBLANES), :] for c in range(SUBLANES)], axis=1)


def _pack_bf16_pair(x):
    w = x.shape[1] // 2
    bits = lax.bitcast_convert_type(x.astype(BF16).astype(F32), jnp.uint32)
    return bits[:, 0:w] | (bits[:, w:] >> 16)


def _unpack_bf16_pair(word):
    hi = lax.bitcast_convert_type(word & jnp.uint32(0xFFFF0000), F32)
    lo = lax.bitcast_convert_type(word << 16, F32)
    return hi, lo


def _norm_matmul_kernel(x_ref, g_ref, w_ref, qkg_ref, seg_ref, o_ref, xn_ref, *, n_qk_tiles):
    j = pl.program_id(1)

    @pl.when(j == 0)
    def _():
        x = x_ref[...]
        inv = lax.rsqrt(jnp.mean(x * x, axis=-1, keepdims=True) + EPS)
        xn_ref[...] = (x * inv * g_ref[...]).astype(BF16)

    acc = jnp.dot(xn_ref[...], w_ref[...], preferred_element_type=F32)

    if n_qk_tiles == 0:
        o_ref[...] = acc.astype(o_ref.dtype)
        return

    @pl.when(j < n_qk_tiles)
    def _():
        sq = (acc * acc).astype(BF16)
        seg = seg_ref[...]
        parts = [jnp.dot(sq[:, c * MXU_DIM:(c + 1) * MXU_DIM], seg, preferred_element_type=F32)
                 for c in range(acc.shape[1] // MXU_DIM)]
        ss = jnp.concatenate(parts, axis=1)
        o_ref[...] = (acc * lax.rsqrt(ss * (1.0 / DQ) + EPS) * qkg_ref[...]).astype(o_ref.dtype)

    @pl.when(j >= n_qk_tiles)
    def _():
        o_ref[...] = acc.astype(o_ref.dtype)


def _norm_matmul(x, g, w, qk_gain=None, n_qk_cols=0):
    m, d = x.shape
    n = w.shape[1]
    tm = min(PROJ_TM, m)
    tn = min(PROJ_TN, n)
    assert m % tm == 0 and n % tn == 0 and n_qk_cols % tn == 0 and tn % MXU_DIM == 0
    if qk_gain is None:
        qk_gain = jnp.ones((1, n), F32)
    r = jnp.arange(MXU_DIM) // DQ
    seg = (r[:, None] == r[None, :]).astype(BF16)
    return pl.pallas_call(
        functools.partial(_norm_matmul_kernel, n_qk_tiles=n_qk_cols // tn),
        grid=(m // tm, n // tn),
        in_specs=[
            pl.BlockSpec((tm, d), lambda i, j: (i, 0)),
            pl.BlockSpec((1, d), lambda i, j: (0, 0)),
            pl.BlockSpec((d, tn), lambda i, j: (0, j)),
            pl.BlockSpec((1, tn), lambda i, j: (0, j)),
            pl.BlockSpec((MXU_DIM, MXU_DIM), lambda i, j: (0, 0)),
        ],
        out_specs=pl.BlockSpec((tm, tn), lambda i, j: (i, j)),
        out_shape=jax.ShapeDtypeStruct((m, n), BF16),
        scratch_shapes=[pltpu.VMEM((tm, d), BF16)],
        compiler_params=_cparams(("parallel", "arbitrary")),
        name="norm_matmul",
    )(x, g.reshape(1, d), w, qk_gain, seg)


def _attn_kernel(q_ref, k_ref, v_ref, boff_ref, bdiag_ref, lam_ref, gs_ref, o_ref,
                 qs_ref, vt_ref, m_ref, l_ref, acc_ref, *, t):
    i = pl.program_id(2)
    s_len = k_ref.shape[1]

    @pl.when(i == 0)
    def _():
        for c in range(s_len // t):
            vt_ref[c] = v_ref[0, c * t:(c + 1) * t, :].astype(F32).T.astype(BF16)

    q = q_ref[0].astype(F32)
    lane = lax.broadcasted_iota(jnp.int32, q.shape, 1)
    qs_ref[0:t, :] = jnp.where(lane < DQ, q, 0.0).astype(BF16)
    qs_ref[t:2 * t, :] = jnp.where(lane >= DQ, q, 0.0).astype(BF16)
    m_ref[...] = jnp.full(m_ref.shape, NEG_INF, F32)
    l_ref[...] = jnp.zeros(l_ref.shape, F32)
    acc_ref[...] = jnp.zeros(acc_ref.shape, F32)

    def scores(j):
        off = pl.multiple_of(j * t, t)
        k = k_ref[0, pl.ds(off, t), :]
        return lax.dot_general(k, qs_ref[...], (((1,), (1,)), ((), ())), preferred_element_type=F32)

    def softmax_pv(j, st):
        m_prev = m_ref[...]
        m_new = jnp.maximum(m_prev, jnp.max(st, axis=0, keepdims=True))
        alpha = jnp.exp2(m_prev - m_new)
        pt = jnp.exp2(st - m_new)
        l_ref[...] = alpha * l_ref[...] + jnp.sum(pt, axis=0, keepdims=True)
        pv = jnp.dot(vt_ref[j], pt.astype(BF16), preferred_element_type=F32)
        acc_ref[...] = alpha * acc_ref[...] + pv
        m_ref[...] = m_new

    def far_body(j, st):
        st_next = scores(j + 1)
        softmax_pv(j, st)
        return st_next

    def off_body(j, st):
        st_next = scores(j + 1)
        softmax_pv(j, st + boff_ref[0])
        return st_next

    st = lax.fori_loop(0, i - 1, far_body, scores(0))
    st = lax.fori_loop(jnp.maximum(i - 1, 0), i, off_body, st)
    softmax_pv(i, st + bdiag_ref[0])

    ot = acc_ref[...] / l_ref[...]
    ot = ot[:, 0:t] - lam_ref[...] * ot[:, t:2 * t]
    inv = lax.rsqrt(jnp.mean(ot * ot, axis=0, keepdims=True) + EPS)
    o_ref[0] = (ot * inv * gs_ref[...]).T.astype(o_ref.dtype)


def _diff_attention(proj, boff, bdiag, lam, gs, n_heads):
    b, s, _ = proj.shape
    t = boff.shape[1]
    assert s % t == 0
    lam_row = jnp.full((1, t), lam, F32)
    gs_col = jnp.broadcast_to(gs.astype(F32)[:, None], (DV, t))
    return pl.pallas_call(
        functools.partial(_attn_kernel, t=t),
        grid=(b, n_heads, s // t),
        in_specs=[
            pl.BlockSpec((1, t, DV), lambda bi, h, i: (bi, i, h)),
            pl.BlockSpec((1, s, DV), lambda bi, h, i: (bi, 0, n_heads + h)),
            pl.BlockSpec((1, s, DV), lambda bi, h, i: (bi, 0, 2 * n_heads + h)),
            pl.BlockSpec((1, t, 2 * t), lambda bi, h, i: (h, 0, 0)),
            pl.BlockSpec((1, t, 2 * t), lambda bi, h, i: (h, 0, 0)),
            pl.BlockSpec((1, t), lambda bi, h, i: (0, 0)),
            pl.BlockSpec((DV, t), lambda bi, h, i: (0, 0)),
        ],
        out_specs=pl.BlockSpec((1, t, DV), lambda bi, h, i: (bi, i, h)),
        out_shape=jax.ShapeDtypeStruct((b, s, n_heads * DV), BF16),
        scratch_shapes=[
            pltpu.VMEM((2 * t, DV), BF16),
            pltpu.VMEM((s // t, DV, t), BF16),
            pltpu.VMEM((1, 2 * t), F32),
            pltpu.VMEM((1, 2 * t), F32),
            pltpu.VMEM((DV, 2 * t), F32),
        ],
        compiler_params=_cparams(("parallel", "parallel", "arbitrary")),
        name="diff_attention",
    )(proj, proj, proj, boff, bdiag, lam_row, gs_col)


def _conv_kernel(a_ref, gate_ref, w_ref, b_ref, lg_ref, lb_ref, o_ref, uext_ref, y_ref, *, ts):
    si = pl.program_id(1)

    @pl.when(si == 0)
    def _():
        uext_ref[0:CONV_HALO, :] = jnp.zeros((CONV_HALO, uext_ref.shape[1]), F32)

    @pl.when(si > 0)
    def _():
        uext_ref[0:CONV_HALO, :] = uext_ref[ts:ts + CONV_HALO, :]

    a = a_ref[0].astype(F32)
    gate = gate_ref[0].astype(F32)
    uext_ref[CONV_HALO:CONV_HALO + ts, :] = a * _sigmoid(gate)

    base = CONV_HALO - (CONV_WIDTH - 1)
    for c in range(uext_ref.shape[1] // LANES):
        cs = slice(c * LANES, (c + 1) * LANES)
        acc = jnp.broadcast_to(b_ref[:, cs], (ts, LANES))
        for j in range(CONV_WIDTH):
            acc = acc + w_ref[j:j + 1, cs] * uext_ref[base + j:base + j + ts, cs]
        y_ref[:, cs] = acc

    y = y_ref[...]
    mu = jnp.mean(y, axis=-1, keepdims=True)
    yc = y - mu
    var = jnp.mean(yc * yc, axis=-1, keepdims=True)
    z = yc * lax.rsqrt(var + EPS) * lg_ref[...] + lb_ref[...]
    o_ref[0] = (z * _sigmoid(z)).astype(o_ref.dtype)


def _conformer_conv(proj, col0, conv_w, conv_b, ln_g, ln_b):
    b, s, _ = proj.shape
    c = conv_w.shape[1]
    ts = min(CONV_TS, s)
    assert s % ts == 0 and col0 % c == 0 and ts >= CONV_HALO
    return pl.pallas_call(
        functools.partial(_conv_kernel, ts=ts),
        grid=(b, s // ts),
        in_specs=[
            pl.BlockSpec((1, ts, c), lambda bi, si: (bi, si, col0 // c)),
            pl.BlockSpec((1, ts, c), lambda bi, si: (bi, si, col0 // c + 1)),
            pl.BlockSpec((CONV_WIDTH, c), lambda bi, si: (0, 0)),
            pl.BlockSpec((1, c), lambda bi, si: (0, 0)),
            pl.BlockSpec((1, c), lambda bi, si: (0, 0)),
            pl.BlockSpec((1, c), lambda bi, si: (0, 0)),
        ],
        out_specs=pl.BlockSpec((1, ts, c), lambda bi, si: (bi, si, 0)),
        out_shape=jax.ShapeDtypeStruct((b, s, c), BF16),
        scratch_shapes=[pltpu.VMEM((CONV_HALO + ts, c), F32), pltpu.VMEM((ts, c), F32)],
        compiler_params=_cparams(("parallel", "arbitrary")),
        name="conformer_conv",
    )(proj, proj, conv_w, conv_b.reshape(1, c), ln_g.reshape(1, c), ln_b.reshape(1, c))


def _outproj_kernel(x_ref, a_ref, c_ref, w_ref, o_ref):
    ka = a_ref.shape[1]
    y = jnp.dot(a_ref[...], w_ref[0:ka, :], preferred_element_type=F32)
    y = y + jnp.dot(c_ref[...], w_ref[ka:, :], preferred_element_type=F32)
    o_ref[...] = x_ref[...] + y


def _outproj_residual(x, attn, conv, w, in_place):
    m, d = x.shape
    ka, kc = attn.shape[1], conv.shape[1]
    tm = min(PROJ_TM, m)
    tn = min(2 * PROJ_TN, d)
    assert m % tm == 0 and d % tn == 0
    return pl.pallas_call(
        _outproj_kernel,
        grid=(m // tm, d // tn),
        in_specs=[
            pl.BlockSpec((tm, tn), lambda i, j: (i, j)),
            pl.BlockSpec((tm, ka), lambda i, j: (i, 0)),
            pl.BlockSpec((tm, kc), lambda i, j: (i, 0)),
            pl.BlockSpec((ka + kc, tn), lambda i, j: (0, j)),
        ],
        out_specs=pl.BlockSpec((tm, tn), lambda i, j: (i, j)),
        out_shape=jax.ShapeDtypeStruct((m, d), F32),
        input_output_aliases={0: 0} if in_place else {},
        compiler_params=_cparams(("parallel", "parallel")),
        name="outproj_residual",
    )(x, attn, conv, w)


def _cross_kernel(x_ref, gc_ref, wq_ref, kv_ref, gq_ref, gk_ref, wo_ref, gf_ref,
                  wrh_ref, wrl_ref, br_ref, xo_ref, hp_ref, rt_ref):
    x = x_ref[0]
    d = x.shape[1]
    tq = x.shape[0]
    h = (x * lax.rsqrt(jnp.mean(x * x, axis=-1, keepdims=True) + EPS) * gc_ref[...]).astype(BF16)
    q = jnp.dot(h, wq_ref[...], preferred_element_type=F32)
    dc = N_HEADS_C * DH_C
    outs = []
    for hd in range(N_HEADS_C):
        cs = slice(hd * DH_C, (hd + 1) * DH_C)
        qh = q[:, cs]
        qh = qh * lax.rsqrt(jnp.mean(qh * qh, axis=-1, keepdims=True) + EPS) * gq_ref[...]
        kh = kv_ref[0, :, cs].astype(F32)
        kh = kh * lax.rsqrt(jnp.mean(kh * kh, axis=-1, keepdims=True) + EPS) * gk_ref[...]
        vh = kv_ref[0, :, dc + hd * DH_C:dc + (hd + 1) * DH_C]
        s = lax.dot_general(qh.astype(BF16), kh.astype(BF16), (((1,), (1,)), ((), ())),
                            preferred_element_type=F32)
        p = jnp.exp(s - jnp.max(s, axis=-1, keepdims=True))
        p = p / jnp.sum(p, axis=-1, keepdims=True)
        outs.append(jnp.dot(p.astype(BF16), vh, preferred_element_type=F32))
    o = jnp.concatenate(outs, axis=1).astype(BF16)
    x2 = x + jnp.dot(o, wo_ref[...], preferred_element_type=F32)
    xo_ref[0] = x2

    h2 = x2 * lax.rsqrt(jnp.mean(x2 * x2, axis=-1, keepdims=True) + EPS) * gf_ref[...]
    h2_hi = h2.astype(BF16)
    h2_hi32 = h2_hi.astype(F32)
    h2_lo = (h2 - h2_hi32).astype(BF16)

    _store_token_major(hp_ref, _pack_bf16_pair(h2_hi32))

    nt = (((1,), (1,)), ((), ()))
    lt = (lax.dot_general(wrh_ref[...], h2_hi, nt, preferred_element_type=F32)
          + lax.dot_general(wrh_ref[...], h2_lo, nt, preferred_element_type=F32)
          + lax.dot_general(wrl_ref[...], h2_hi, nt, preferred_element_type=F32)) + br_ref[...]
    g, e = N_GROUPS, EXPERTS_PER_GROUP
    gl = lt[0:g]
    rowg = lax.broadcasted_iota(jnp.int32, gl.shape, 0)
    gmax = jnp.max(gl, axis=0, keepdims=True)
    gsel = jnp.min(jnp.where(gl == gmax, rowg, g), axis=0, keepdims=True)
    gw = 1.0 / jnp.sum(jnp.exp(gl - gmax), axis=0, keepdims=True)
    el = jnp.zeros((e, tq), F32)
    for gi in range(g):
        el = jnp.where(gsel == gi, lt[g + gi * e:g + (gi + 1) * e], el)
    rowe = lax.broadcasted_iota(jnp.int32, el.shape, 0)
    v1 = jnp.max(el, axis=0, keepdims=True)
    i1 = jnp.min(jnp.where(el == v1, rowe, e), axis=0, keepdims=True)
    el2 = jnp.where(rowe == i1, -jnp.inf, el)
    v2 = jnp.max(el2, axis=0, keepdims=True)
    i2 = jnp.min(jnp.where(el2 == v2, rowe, e), axis=0, keepdims=True)
    e2 = jnp.exp(v2 - v1)
    den = 1.0 / (1.0 + e2)
    id1 = (gsel * e + i1).astype(F32)
    id2 = (gsel * e + i2).astype(F32)
    row8 = lax.broadcasted_iota(jnp.int32, (8, tq), 0)
    rt = jnp.where(row8 == 0, id1, jnp.where(row8 == 1, id2,
         jnp.where(row8 == 2, gw * den, jnp.where(row8 == 3, gw * e2 * den, 0.0))))
    rt_ref[0] = rt


def _cross_block(x, kv, g_cross, wq, g_qc, g_kc, wo, g_ffn, wr_hi, wr_lo, b_r):
    b, s, d = x.shape
    mlen = kv.shape[1]
    tq = min(CROSS_TQ, s)
    nq = s // tq
    dc = N_HEADS_C * DH_C
    nr = wr_hi.shape[0]
    assert s % tq == 0
    const = lambda bi, qi: (0, 0)
    return pl.pallas_call(
        _cross_kernel,
        grid=(b, nq),
        in_specs=[
            pl.BlockSpec((1, tq, d), lambda bi, qi: (bi, qi, 0)),
            pl.BlockSpec((1, d), const),
            pl.BlockSpec((d, dc), const),
            pl.BlockSpec((1, mlen, 2 * dc), lambda bi, qi: (bi, 0, 0)),
            pl.BlockSpec((1, DH_C), const),
            pl.BlockSpec((1, DH_C), const),
            pl.BlockSpec((dc, d), const),
            pl.BlockSpec((1, d), const),
            pl.BlockSpec((nr, d), const),
            pl.BlockSpec((nr, d), const),
            pl.BlockSpec((nr, 1), const),
        ],
        out_specs=[
            pl.BlockSpec((1, tq, d), lambda bi, qi: (bi, qi, 0)),
            pl.BlockSpec((tq * SUBLANES, LANES), lambda bi, qi: (bi * nq + qi, 0)),
            pl.BlockSpec((1, 8, tq), lambda bi, qi: (bi * nq + qi, 0, 0)),
        ],
        out_shape=[
            jax.ShapeDtypeStruct((b, s, d), F32),
            jax.ShapeDtypeStruct((b * s * SUBLANES, LANES), jnp.uint32),
            jax.ShapeDtypeStruct((b * nq, 8, tq), F32),
        ],
        input_output_aliases={0: 0},
        compiler_params=_cparams(("parallel", "parallel")),
        name="cross_block",
    )(x, g_cross.reshape(1, d), wq, kv, (g_qc * DH_C ** -0.5).reshape(1, DH_C), g_kc.reshape(1, DH_C),
      wo, g_ffn.reshape(1, d), wr_hi, wr_lo, b_r)


def _expert_kernel(idx_ref, be_ref, nr_ref, h_hbm, w1_ref, w3_ref, w2_ref, g_ref, z_hbm,
                   xbuf0, xbuf1, ybuf0, ybuf1, w1b, w3b, w2b, gsem, ssem, *, blk, n_steps):
    i = pl.program_id(0)
    n_cur = nr_ref[i]
    i_prev = jnp.maximum(i - 1, 0)
    n_prev = jnp.where(i >= 1, nr_ref[i_prev], 0)
    slot = i % 2
    tok_mask = (1 << TOK_BITS) - 1
    xbuf = (xbuf0, xbuf1)
    ybuf = (ybuf0, ybuf1)
    rs = SUBLANES

    def tile_rows(row):
        return pl.ds(pl.multiple_of(row * rs, rs), rs)

    def gather_copy(step, r, sl):
        row = idx_ref[step * blk + r] & tok_mask
        return pltpu.make_async_copy(h_hbm.at[tile_rows(row)], xbuf[sl].at[tile_rows(r)], gsem.at[sl])

    def scatter_row(step, r):
        return lax.shift_right_logical(idx_ref[step * blk + r], TOK_BITS)

    def scatter_copy(row, r, sl):
        return pltpu.make_async_copy(ybuf[sl].at[tile_rows(r)], z_hbm.at[tile_rows(row)], ssem.at[sl])

    def issue_gather(step, sl):
        def body(r, carry):
            gather_copy(step, r, sl).start()
            return carry
        lax.fori_loop(0, blk, body, 0, unroll=8)

    def wait_gather(sl):
        pltpu.make_async_copy(h_hbm.at[pl.ds(0, blk * rs)], xbuf[sl], gsem.at[sl]).wait()

    def issue_scatter(step, sl, n):
        def body(r, carry):
            row = scatter_row(step, r)

            @pl.when(r < n)
            def _():
                scatter_copy(row, r, sl).start()
            return carry
        lax.fori_loop(0, blk, body, 0, unroll=8)

    def wait_scatter(sl, n):
        k = blk
        while k >= 1:
            @pl.when((n & k) != 0)
            def _(k=k):
                pltpu.make_async_copy(ybuf[sl].at[pl.ds(0, k * rs)], z_hbm.at[pl.ds(0, k * rs)],
                                      ssem.at[sl]).wait()
            k //= 2

    @pl.when(i == 0)
    def _():
        issue_gather(0, 0)

    @pl.when((n_cur > 0) & ((i == 0) | (be_ref[i] != be_ref[jnp.maximum(i - 1, 0)])))
    def _():
        w1b[...] = w1_ref[0, 0].astype(BF16)
        w3b[...] = w3_ref[0, 0].astype(BF16)
        w2b[...] = w2_ref[0, 0].astype(BF16)

    def block_body(sl):
        wait_gather(sl)
        xl, xr = _unpack_bf16_pair(_load_token_major(xbuf[sl], blk))
        xl, xr = xl.astype(BF16), xr.astype(BF16)
        dh = xl.shape[1]
        for r in range(blk):
            gather_copy(i + 1, r, 1 - sl).start()
        for r in range(blk):
            row = scatter_row(i_prev, r)

            @pl.when(r < n_prev)
            def _(r=r, row=row):
                scatter_copy(row, r, 1 - sl).start()
        a1 = (jnp.dot(xl, w1b[0:dh, :], preferred_element_type=F32)
              + jnp.dot(xr, w1b[dh:, :], preferred_element_type=F32))
        a3 = (jnp.dot(xl, w3b[0:dh, :], preferred_element_type=F32)
              + jnp.dot(xr, w3b[dh:, :], preferred_element_type=F32))
        hid = (a1 * _sigmoid(a1) * a3 * g_ref[...]).astype(BF16)
        y = jnp.dot(hid, w2b[...], preferred_element_type=F32)
        _store_token_major(ybuf[sl], _pack_bf16_pair(y))
        wait_scatter(1 - sl, n_prev)

        @pl.when(i == n_steps - 1)
        def _():
            wait_gather(1 - sl)
            issue_scatter(i, sl, n_cur)
            wait_scatter(sl, n_cur)

    for sl in range(2):
        @pl.when((n_cur > 0) & (slot == sl))
        def _(sl=sl):
            block_body(sl)

    for sl in range(2):
        @pl.when((n_cur == 0) & (n_prev > 0) & (slot == sl))
        def _(sl=sl):
            wait_gather(sl)
            issue_scatter(i_prev, 1 - sl, n_prev)
            wait_scatter(1 - sl, n_prev)


def _expert_mlp(hpack, idx, blk_expert, nreal, gate_buf, w1, w3, w2, layer, z_rows):
    d = w1.shape[2]
    assert d == 2 * SUBLANES * LANES and hpack.shape[1] == LANES
    n_steps = idx.shape[0] // MOE_BLK - 1
    de = w1.shape[3]
    assert MOE_BLK & (MOE_BLK - 1) == 0
    return pl.pallas_call(
        functools.partial(_expert_kernel, blk=MOE_BLK, n_steps=n_steps),
        grid_spec=pltpu.PrefetchScalarGridSpec(
            num_scalar_prefetch=3,
            grid=(n_steps,),
            in_specs=[
                pl.BlockSpec(memory_space=pl.ANY),
                pl.BlockSpec((1, 1, d, de), lambda i, ix, be, nr: (layer, be[i], 0, 0)),
                pl.BlockSpec((1, 1, d, de), lambda i, ix, be, nr: (layer, be[i], 0, 0)),
                pl.BlockSpec((1, 1, de, d), lambda i, ix, be, nr: (layer, be[i], 0, 0)),
                pl.BlockSpec((MOE_BLK, 1), lambda i, ix, be, nr: (i, 0)),
            ],
            out_specs=pl.BlockSpec(memory_space=pl.ANY),
            scratch_shapes=[
                pltpu.VMEM((MOE_BLK * SUBLANES, LANES), jnp.uint32),
                pltpu.VMEM((MOE_BLK * SUBLANES, LANES), jnp.uint32),
                pltpu.VMEM((MOE_BLK * SUBLANES, LANES), jnp.uint32),
                pltpu.VMEM((MOE_BLK * SUBLANES, LANES), jnp.uint32),
                pltpu.VMEM((d, de), BF16),
                pltpu.VMEM((d, de), BF16),
                pltpu.VMEM((de, d), BF16),
                pltpu.SemaphoreType.DMA((2,)),
                pltpu.SemaphoreType.DMA((2,)),
            ],
        ),
        out_shape=jax.ShapeDtypeStruct((z_rows * SUBLANES, LANES), jnp.uint32),
        compiler_params=_cparams(("arbitrary",)),
        name="expert_mlp",
    )(idx, blk_expert, nreal, hpack, w1, w3, w2, gate_buf)


def _combine_kernel(x_ref, *refs):
    z_refs, o_ref = refs[:-1], refs[-1]
    tm, d = x_ref.shape
    acc_hi = x_ref[:, 0:d // 2]
    acc_lo = x_ref[:, d // 2:]
    for z_ref in z_refs:
        hi, lo = _unpack_bf16_pair(_load_token_major(z_ref, tm))
        acc_hi = acc_hi + hi
        acc_lo = acc_lo + lo
    o_ref[:, 0:d // 2] = acc_hi
    o_ref[:, d // 2:] = acc_lo


def _combine(x, z):
    t, d = x.shape
    tm = min(COMB_TM, t)
    assert t % tm == 0
    nt = t // tm
    return pl.pallas_call(
        _combine_kernel,
        grid=(nt,),
        in_specs=[pl.BlockSpec((tm, d), lambda i: (i, 0))] + [
            pl.BlockSpec((tm * SUBLANES, LANES), lambda i, k=k: (k * nt + i, 0)) for k in range(TOP_K)],
        out_specs=pl.BlockSpec((tm, d), lambda i: (i, 0)),
        out_shape=jax.ShapeDtypeStruct((t, d), F32),
        input_output_aliases={0: 0},
        compiler_params=_cparams(("parallel",)),
        name="moe_combine",
    )(x, *([z] * TOP_K))


def _t5_causal_bucket(dist):
    max_exact = N_BUCKETS // 2
    d_f = jnp.maximum(dist, 1).astype(F32)
    large = max_exact + (jnp.log(d_f / max_exact) / math.log(MAX_DISTANCE / max_exact)
                         * (N_BUCKETS - max_exact)).astype(jnp.int32)
    large = jnp.minimum(large, N_BUCKETS - 1)
    return jnp.where(dist < max_exact, dist, large)


def _bias_tiles(rel_bias_table, s, t):
    assert t >= MAX_DISTANCE
    rbd = rel_bias_table[_t5_causal_bucket(jnp.arange(s))].T.astype(F32)
    far = rel_bias_table[N_BUCKETS - 1].astype(F32)
    rbd = rbd - far[:, None]
    h = rbd.shape[0]

    def toeplitz(w):
        a = jnp.broadcast_to(w[:, None, :], (h, t, 2 * t)).reshape(h, 2 * t * t)
        return a[:, :t * (2 * t - 1)].reshape(h, t, 2 * t - 1)[:, :, :t]

    k = np.arange(2 * t)
    d_diag = np.where(k == 0, 0, np.minimum(2 * t - k, s - 1))
    w_diag = jnp.where((k >= 1) & (k <= t), NEG_INF, rbd[:, d_diag])
    d_off = np.minimum(np.where(k < t, t - k, 3 * t - k), s - 1)
    w_off = rbd[:, d_off]
    off = toeplitz(w_off).transpose(0, 2, 1) * LOG2E
    diag = jnp.maximum(toeplitz(w_diag).transpose(0, 2, 1) * LOG2E, NEG_INF)
    return jnp.concatenate([off, off], axis=2), jnp.concatenate([diag, diag], axis=2)


def _moe_plan(rt, t, n_experts, blk):
    nblk_rt, _, tq = rt.shape
    ids = rt[:, 0:TOP_K, :].astype(jnp.int32).transpose(0, 2, 1).reshape(t * TOP_K)
    gates = rt[:, TOP_K:2 * TOP_K, :].transpose(0, 2, 1).reshape(t * TOP_K)
    a = t * TOP_K
    order = jnp.argsort(ids).astype(jnp.int32)
    counts = jnp.sum((ids[None, :] == jnp.arange(n_experts)[:, None]).astype(jnp.int32), axis=1)
    start = jnp.cumsum(counts) - counts
    pcounts = (counts + blk - 1) // blk * blk
    pend = jnp.cumsum(pcounts)
    pstart = pend - pcounts
    n_blocks = -(-(a + n_experts * (blk - 1)) // blk)
    p = n_blocks * blk
    blk_row0 = jnp.arange(n_blocks, dtype=jnp.int32) * blk
    blk_expert = jnp.minimum(jnp.sum((pend[None, :] <= blk_row0[:, None]).astype(jnp.int32), axis=1),
                             n_experts - 1)
    nreal = jnp.clip((pstart + counts)[blk_expert] - blk_row0, 0, blk).astype(jnp.int32)
    within = (blk_row0 - pstart[blk_expert])[:, None] + jnp.arange(blk, dtype=jnp.int32)[None, :]
    valid = jnp.arange(blk, dtype=jnp.int32)[None, :] < nreal[:, None]
    src = jnp.where(valid, start[blk_expert][:, None] + within, 0).reshape(p)
    valid = valid.reshape(p)
    asg = order[src].astype(jnp.uint32)
    zrow = (asg % TOP_K) * t + asg // TOP_K
    idx = jnp.where(valid, (zrow << TOK_BITS) | (asg // TOP_K), 0).astype(jnp.uint32)
    idx = lax.bitcast_convert_type(idx, jnp.int32)
    idx = jnp.concatenate([idx, jnp.zeros((blk,), jnp.int32)])
    gate_buf = jnp.where(valid, gates[order[src]], 0.0)
    nreal = jnp.concatenate([nreal, jnp.zeros((1,), jnp.int32)])
    return idx, blk_expert.astype(jnp.int32), nreal, gate_buf.reshape(p, 1)


def kernel(x, mem, rel_bias_table, g_mix, w_in, g_q, g_k, diff_lambda, g_subln, conv_w, conv_b,
           conv_ln_g, conv_ln_b, w_out, g_cross, g_mem, wq_c, wkv_c, g_qc, g_kc, wo_c, g_ffn,
           w_group, b_group, w_router, b_router, w1, w3, w2):
    b, s, d = x.shape
    mlen = mem.shape[1]
    depth = w_in.shape[0]
    t = b * s
    d_conv = conv_w.shape[2]
    n_heads = (w_out.shape[1] - d_conv) // DV
    d_qk = n_heads * 2 * DQ
    n_experts = w_router.shape[2]
    assert t <= (1 << TOK_BITS) and TOP_K * t <= (1 << (32 - TOK_BITS))

    boff, bdiag = _bias_tiles(rel_bias_table, s, min(ATTN_T, s))
    n_router_rows = -(-(N_GROUPS + n_experts) // LANES) * LANES

    for l in range(depth):
        lam_init = 0.8 - 0.6 * math.exp(-0.3 * l)
        dl = diff_lambda[l].astype(F32)
        lam = jnp.exp(jnp.sum(dl[0] * dl[1])) - jnp.exp(jnp.sum(dl[2] * dl[3])) + lam_init
        gs = g_subln[l] * (1.0 - lam_init)
        qk_gain = jnp.concatenate([
            jnp.tile(g_q[l] * (DQ ** -0.5 * LOG2E), d_qk // DQ), jnp.tile(g_k[l], d_qk // DQ),
            jnp.ones((w_in.shape[2] - 2 * d_qk,), F32)]).reshape(1, -1)

        proj = _norm_matmul(x.reshape(t, d), g_mix[l], w_in[l].astype(BF16), qk_gain, 2 * d_qk)
        proj = proj.reshape(b, s, -1)
        attn = _diff_attention(proj, boff, bdiag, lam, gs, n_heads)
        conv = _conformer_conv(proj, 2 * d_qk + n_heads * DV, conv_w[l], conv_b[l],
                               conv_ln_g[l], conv_ln_b[l])
        x = _outproj_residual(x.reshape(t, d), attn.reshape(t, -1), conv.reshape(t, -1),
                              w_out[l].astype(BF16), in_place=l > 0).reshape(b, s, d)

        kv = _norm_matmul(mem.reshape(b * mlen, d), g_mem[l], wkv_c[l].astype(BF16))
        wr = jnp.zeros((n_router_rows, d), F32)
        wr = wr.at[0:N_GROUPS].set(w_group[l].T).at[N_GROUPS:N_GROUPS + n_experts].set(w_router[l].T)
        wr_hi = wr.astype(BF16)
        wr_lo = (wr - wr_hi.astype(F32)).astype(BF16)
        b_r = jnp.zeros((n_router_rows, 1), F32)
        b_r = b_r.at[0:N_GROUPS, 0].set(b_group[l]).at[N_GROUPS:N_GROUPS + n_experts, 0].set(b_router[l])
        x, hpack, rt = _cross_block(x, kv.reshape(b, mlen, -1), g_cross[l], wq_c[l].astype(BF16),
                                    g_qc[l], g_kc[l], wo_c[l].astype(BF16), g_ffn[l], wr_hi, wr_lo, b_r)

        idx, blk_expert, nreal, gate_buf = _moe_plan(rt, t, n_experts, MOE_BLK)
        z = _expert_mlp(hpack, idx, blk_expert, nreal, gate_buf, w1, w3, w2, l, TOP_K * t)
        x = _combine(x.reshape(t, d), z).reshape(b, s, d)
    return x
```

```python
import functools
import math

import jax
import jax.numpy as jnp
import numpy as np
from jax import lax
from jax.experimental import pallas as pl
from jax.experimental.pallas import tpu as pltpu

DQ = 64
DV = 2 * DQ
CONV_WIDTH = 31
N_BUCKETS = 32
MAX_DISTANCE = 128
N_HEADS_C = 4
DH_C = 128
N_GROUPS = 8
EXPERTS_PER_GROUP = 8
TOP_K = 2
EPS = 1e-6
NEG_INF = -1e30
LOG2E = 1.4426950408889634
ATTN_SAFE_LOG2 = 96.0

LANES = 128
SUBLANES = 8
MXU_DIM = 256
VMEM_LIMIT_BYTES = 56 * 1024 * 1024

F32 = jnp.float32
BF16 = jnp.bfloat16

PROJ_TM = 1024
PROJ_TN = 512
ATTN_T = 256
CONV_TS = 256
CONV_HALO = 32
CROSS_TQ = 256
MOE_BLK = 256
COMB_TM = 256
TOK_BITS = 15


def _cparams(sem):
    return pltpu.CompilerParams(dimension_semantics=sem, vmem_limit_bytes=VMEM_LIMIT_BYTES)


def _sigmoid(x):
    return 1.0 / (1.0 + jnp.exp(-x))


def _store_token_major(ref, val):
    n = val.shape[0]
    for c in range(SUBLANES):
        ref[pl.ds(c, n, stride=SUBLANES), :] = val[:, c * LANES:(c + 1) * LANES]


def _load_token_major(ref, n):
    return jnp.concatenate([ref[pl.ds(c, n, stride=SUBLANES), :] for c in range(SUBLANES)], axis=1)


def _pack_bf16_pair(x):
    w = x.shape[1] // 2
    bits = lax.bitcast_convert_type(x.astype(BF16).astype(F32), jnp.uint32)
    return bits[:, 0:w] | (bits[:, w:] >> 16)


def _unpack_bf16_pair(word):
    hi = lax.bitcast_convert_type(word & jnp.uint32(0xFFFF0000), F32)
    lo = lax.bitcast_convert_type(word << 16, F32)
    return hi, lo


def _norm_matmul_kernel(x_ref, g_ref, w_ref, qkg_ref, seg_ref, o_ref, xn_ref, *, n_qk_tiles):
    j = pl.program_id(1)

    @pl.when(j == 0)
    def _():
        x = x_ref[...]
        inv = lax.rsqrt(jnp.mean(x * x, axis=-1, keepdims=True) + EPS)
        xn_ref[...] = (x * inv * g_ref[...]).astype(BF16)

    acc = jnp.dot(xn_ref[...], w_ref[...], preferred_element_type=F32)

    if n_qk_tiles == 0:
        o_ref[...] = acc.astype(o_ref.dtype)
        return

    @pl.when(j < n_qk_tiles)
    def _():
        sq = (acc * acc).astype(BF16)
        seg = seg_ref[...]
        parts = [jnp.dot(sq[:, c * MXU_DIM:(c + 1) * MXU_DIM], seg, preferred_element_type=F32)
                 for c in range(acc.shape[1] // MXU_DIM)]
        ss = jnp.concatenate(parts, axis=1)
        o_ref[...] = (acc * lax.rsqrt(ss * (1.0 / DQ) + EPS) * qkg_ref[...]).astype(o_ref.dtype)

    @pl.when(j >= n_qk_tiles)
    def _():
        o_ref[...] = acc.astype(o_ref.dtype)


def _norm_matmul(x, g, w, qk_gain=None, n_qk_cols=0):
    m, d = x.shape
    n = w.shape[1]
    tm = min(PROJ_TM, m)
    tn = min(PROJ_TN, n)
    assert m % tm == 0 and n % tn == 0 and n_qk_cols % tn == 0 and tn % MXU_DIM == 0
    if qk_gain is None:
        qk_gain = jnp.ones((1, n), F32)
    r = jnp.arange(MXU_DIM) // DQ
    seg = (r[:, None] == r[None, :]).astype(BF16)
    return pl.pallas_call(
        functools.partial(_norm_matmul_kernel, n_qk_tiles=n_qk_cols // tn),
        grid=(m // tm, n // tn),
        in_specs=[
            pl.BlockSpec((tm, d), lambda i, j: (i, 0)),
            pl.BlockSpec((1, d), lambda i, j: (0, 0)),
            pl.BlockSpec((d, tn), lambda i, j: (0, j)),
            pl.BlockSpec((1, tn), lambda i, j: (0, j)),
            pl.BlockSpec((MXU_DIM, MXU_DIM), lambda i, j: (0, 0)),
        ],
        out_specs=pl.BlockSpec((tm, tn), lambda i, j: (i, j)),
        out_shape=jax.ShapeDtypeStruct((m, n), BF16),
        scratch_shapes=[pltpu.VMEM((tm, d), BF16)],
        compiler_params=_cparams(("parallel", "arbitrary")),
        name="norm_matmul",
    )(x, g.reshape(1, d), w, qk_gain, seg)


def _attn_kernel(q_ref, k_ref, v_ref, boff_ref, bdiag_ref, lam_ref, gs_ref, o_ref,
                 qs_ref, vt_ref, m_ref, l_ref, acc_ref, *, t, bounded):
    i = pl.program_id(2)
    s_len = k_ref.shape[1]

    @pl.when(i == 0)
    def _():
        for c in range(s_len // t):
            vt_ref[c] = v_ref[0, c * t:(c + 1) * t, :].astype(F32).T.astype(BF16)

    q = q_ref[0].astype(F32)
    lane = lax.broadcasted_iota(jnp.int32, q.shape, 1)
    qs_ref[0:t, :] = jnp.where(lane < DQ, q, 0.0).astype(BF16)
    qs_ref[t:2 * t, :] = jnp.where(lane >= DQ, q, 0.0).astype(BF16)
    def scores(j):
        off = j * t if isinstance(j, int) else pl.multiple_of(j * t, t)
        k = k_ref[0, pl.ds(off, t), :]
        return lax.dot_general(k, qs_ref[...], (((1,), (1,)), ((), ())), preferred_element_type=F32)

    def finish(acc, l):
        ot = acc / l
        ot = ot[:, 0:t] - lam_ref[...] * ot[:, t:2 * t]
        inv = lax.rsqrt(jnp.mean(ot * ot, axis=0, keepdims=True) + EPS)
        o_ref[0] = (ot * inv * gs_ref[...]).T.astype(o_ref.dtype)

    if bounded:
        def sweep(ii):
            acc = l8 = None
            for j in range(ii + 1):
                st = scores(j)
                if j == ii:
                    st = st + bdiag_ref[0]
                elif j == ii - 1:
                    st = st + boff_ref[0]
                pt = jnp.exp2(st)
                lj = jnp.sum(pt.reshape(t // SUBLANES, SUBLANES, 2 * t), axis=0)
                pv = jnp.dot(vt_ref[j], pt.astype(BF16), preferred_element_type=F32)
                acc = pv if acc is None else acc + pv
                l8 = lj if l8 is None else l8 + lj
            finish(acc, jnp.sum(l8, axis=0, keepdims=True))

        for ii in range(s_len // t):
            pl.when(i == ii)(functools.partial(sweep, ii))
        return

    m_ref[...] = jnp.full(m_ref.shape, NEG_INF, F32)
    l_ref[...] = jnp.zeros(l_ref.shape, F32)
    acc_ref[...] = jnp.zeros(acc_ref.shape, F32)

    def softmax_pv(j, st):
        m_prev = m_ref[...]
        m_new = jnp.maximum(m_prev, jnp.max(st, axis=0, keepdims=True))
        alpha = jnp.exp2(m_prev - m_new)
        pt = jnp.exp2(st - m_new)
        l_ref[...] = alpha * l_ref[...] + jnp.sum(pt, axis=0, keepdims=True)
        pv = jnp.dot(vt_ref[j], pt.astype(BF16), preferred_element_type=F32)
        acc_ref[...] = alpha * acc_ref[...] + pv
        m_ref[...] = m_new

    def far_body(j, st):
        st_next = scores(j + 1)
        softmax_pv(j, st)
        return st_next

    def off_body(j, st):
        st_next = scores(j + 1)
        softmax_pv(j, st + boff_ref[0])
        return st_next

    st = lax.fori_loop(0, i - 1, far_body, scores(0))
    st = lax.fori_loop(jnp.maximum(i - 1, 0), i, off_body, st)
    softmax_pv(i, st + bdiag_ref[0])
    finish(acc_ref[...], l_ref[...])


def _diff_attention(proj, boff, bdiag, lam, gs, n_heads, score_bound):
    b, s, _ = proj.shape
    t = boff.shape[1]
    assert s % t == 0
    lam_row = jnp.full((1, t), lam, F32)
    gs_col = jnp.broadcast_to(gs.astype(F32)[:, None], (DV, t))
    operands = (proj, proj, proj, boff, bdiag, lam_row, gs_col)
    return lax.cond(score_bound <= ATTN_SAFE_LOG2,
                    functools.partial(_attn_call, t=t, n_heads=n_heads, bounded=True),
                    functools.partial(_attn_call, t=t, n_heads=n_heads, bounded=False),
                    *operands)


def _attn_call(*operands, t, n_heads, bounded):
    b, s, _ = operands[0].shape
    return pl.pallas_call(
        functools.partial(_attn_kernel, t=t, bounded=bounded),
        grid=(b, n_heads, s // t),
        in_specs=[
            pl.BlockSpec((1, t, DV), lambda bi, h, i: (bi, i, h)),
            pl.BlockSpec((1, s, DV), lambda bi, h, i: (bi, 0, n_heads + h)),
            pl.BlockSpec((1, s, DV), lambda bi, h, i: (bi, 0, 2 * n_heads + h)),
            pl.BlockSpec((1, t, 2 * t), lambda bi, h, i: (h, 0, 0)),
            pl.BlockSpec((1, t, 2 * t), lambda bi, h, i: (h, 0, 0)),
            pl.BlockSpec((1, t), lambda bi, h, i: (0, 0)),
            pl.BlockSpec((DV, t), lambda bi, h, i: (0, 0)),
        ],
        out_specs=pl.BlockSpec((1, t, DV), lambda bi, h, i: (bi, i, h)),
        out_shape=jax.ShapeDtypeStruct((b, s, n_heads * DV), BF16),
        scratch_shapes=[
            pltpu.VMEM((2 * t, DV), BF16),
            pltpu.VMEM((s // t, DV, t), BF16),
            pltpu.VMEM((1, 2 * t), F32),
            pltpu.VMEM((1, 2 * t), F32),
            pltpu.VMEM((DV, 2 * t), F32),
        ],
        compiler_params=_cparams(("parallel", "parallel", "arbitrary")),
        name="diff_attention_bounded" if bounded else "diff_attention",
    )(*operands)


def _conv_kernel(a_ref, gate_ref, w_ref, b_ref, lg_ref, lb_ref, o_ref, uext_ref, y_ref, *, ts):
    si = pl.program_id(1)

    @pl.when(si == 0)
    def _():
        uext_ref[0:CONV_HALO, :] = jnp.zeros((CONV_HALO, uext_ref.shape[1]), F32)

    @pl.when(si > 0)
    def _():
        uext_ref[0:CONV_HALO, :] = uext_ref[ts:ts + CONV_HALO, :]

    a = a_ref[0].astype(F32)
    gate = gate_ref[0].astype(F32)
    uext_ref[CONV_HALO:CONV_HALO + ts, :] = a * _sigmoid(gate)

    base = CONV_HALO - (CONV_WIDTH - 1)
    for c in range(uext_ref.shape[1] // LANES):
        cs = slice(c * LANES, (c + 1) * LANES)
        acc = jnp.broadcast_to(b_ref[:, cs], (ts, LANES))
        for j in range(CONV_WIDTH):
            acc = acc + w_ref[j:j + 1, cs] * uext_ref[base + j:base + j + ts, cs]
        y_ref[:, cs] = acc

    y = y_ref[...]
    mu = jnp.mean(y, axis=-1, keepdims=True)
    yc = y - mu
    var = jnp.mean(yc * yc, axis=-1, keepdims=True)
    z = yc * lax.rsqrt(var + EPS) * lg_ref[...] + lb_ref[...]
    o_ref[0] = (z * _sigmoid(z)).astype(o_ref.dtype)


def _conformer_conv(proj, col0, conv_w, conv_b, ln_g, ln_b):
    b, s, _ = proj.shape
    c = conv_w.shape[1]
    ts = min(CONV_TS, s)
    assert s % ts == 0 and col0 % c == 0 and ts >= CONV_HALO
    return pl.pallas_call(
        functools.partial(_conv_kernel, ts=ts),
        grid=(b, s // ts),
        in_specs=[
            pl.BlockSpec((1, ts, c), lambda bi, si: (bi, si, col0 // c)),
            pl.BlockSpec((1, ts, c), lambda bi, si: (bi, si, col0 // c + 1)),
            pl.BlockSpec((CONV_WIDTH, c), lambda bi, si: (0, 0)),
            pl.BlockSpec((1, c), lambda bi, si: (0, 0)),
            pl.BlockSpec((1, c), lambda bi, si: (0, 0)),
            pl.BlockSpec((1, c), lambda bi, si: (0, 0)),
        ],
        out_specs=pl.BlockSpec((1, ts, c), lambda bi, si: (bi, si, 0)),
        out_shape=jax.ShapeDtypeStruct((b, s, c), BF16),
        scratch_shapes=[pltpu.VMEM((CONV_HALO + ts, c), F32), pltpu.VMEM((ts, c), F32)],
        compiler_params=_cparams(("parallel", "arbitrary")),
        name="conformer_conv",
    )(proj, proj, conv_w, conv_b.reshape(1, c), ln_g.reshape(1, c), ln_b.reshape(1, c))


def _outproj_kernel(x_ref, a_ref, c_ref, w_ref, o_ref):
    ka = a_ref.shape[1]
    y = jnp.dot(a_ref[...], w_ref[0:ka, :], preferred_element_type=F32)
    y = y + jnp.dot(c_ref[...], w_ref[ka:, :], preferred_element_type=F32)
    o_ref[...] = x_ref[...] + y


def _outproj_residual(x, attn, conv, w, in_place):
    m, d = x.shape
    ka, kc = attn.shape[1], conv.shape[1]
    tm = min(PROJ_TM, m)
    tn = min(2 * PROJ_TN, d)
    assert m % tm == 0 and d % tn == 0
    return pl.pallas_call(
        _outproj_kernel,
        grid=(m // tm, d // tn),
        in_specs=[
            pl.BlockSpec((tm, tn), lambda i, j: (i, j)),
            pl.BlockSpec((tm, ka), lambda i, j: (i, 0)),
            pl.BlockSpec((tm, kc), lambda i, j: (i, 0)),
            pl.BlockSpec((ka + kc, tn), lambda i, j: (0, j)),
        ],
        out_specs=pl.BlockSpec((tm, tn), lambda i, j: (i, j)),
        out_shape=jax.ShapeDtypeStruct((m, d), F32),
        input_output_aliases={0: 0} if in_place else {},
        compiler_params=_cparams(("parallel", "parallel")),
        name="outproj_residual",
    )(x, attn, conv, w)


def _cross_kernel(x_ref, gc_ref, wq_ref, kv_ref, gq_ref, gk_ref, wo_ref, gf_ref,
                  wrh_ref, wrl_ref, br_ref, xo_ref, hp_ref, rt_ref):
    x = x_ref[0]
    d = x.shape[1]
    tq = x.shape[0]
    h = (x * lax.rsqrt(jnp.mean(x * x, axis=-1, keepdims=True) + EPS) * gc_ref[...]).astype(BF16)
    q = jnp.dot(h, wq_ref[...], preferred_element_type=F32)
    dc = N_HEADS_C * DH_C
    outs = []
    for hd in range(N_HEADS_C):
        cs = slice(hd * DH_C, (hd + 1) * DH_C)
        qh = q[:, cs]
        qh = qh * lax.rsqrt(jnp.mean(qh * qh, axis=-1, keepdims=True) + EPS) * gq_ref[...]
        kh = kv_ref[0, :, cs].astype(F32)
        kh = kh * lax.rsqrt(jnp.mean(kh * kh, axis=-1, keepdims=True) + EPS) * gk_ref[...]
        vh = kv_ref[0, :, dc + hd * DH_C:dc + (hd + 1) * DH_C]
        s = lax.dot_general(qh.astype(BF16), kh.astype(BF16), (((1,), (1,)), ((), ())),
                            preferred_element_type=F32)
        p = jnp.exp(s - jnp.max(s, axis=-1, keepdims=True))
        p = p / jnp.sum(p, axis=-1, keepdims=True)
        outs.append(jnp.dot(p.astype(BF16), vh, preferred_element_type=F32))
    o = jnp.concatenate(outs, axis=1).astype(BF16)
    x2 = x + jnp.dot(o, wo_ref[...], preferred_element_type=F32)
    xo_ref[0] = x2

    h2 = x2 * lax.rsqrt(jnp.mean(x2 * x2, axis=-1, keepdims=True) + EPS) * gf_ref[...]
    h2_hi = h2.astype(BF16)
    h2_hi32 = h2_hi.astype(F32)
    h2_lo = (h2 - h2_hi32).astype(BF16)

    _store_token_major(hp_ref, _pack_bf16_pair(h2_hi32))

    nt = (((1,), (1,)), ((), ()))
    lt = (lax.dot_general(wrh_ref[...], h2_hi, nt, preferred_element_type=F32)
          + lax.dot_general(wrh_ref[...], h2_lo, nt, preferred_element_type=F32)
          + lax.dot_general(wrl_ref[...], h2_hi, nt, preferred_element_type=F32)) + br_ref[...]
    g, e = N_GROUPS, EXPERTS_PER_GROUP
    gl = lt[0:g]
    rowg = lax.broadcasted_iota(jnp.int32, gl.shape, 0)
    gmax = jnp.max(gl, axis=0, keepdims=True)
    gsel = jnp.min(jnp.where(gl == gmax, rowg, g), axis=0, keepdims=True)
    gw = 1.0 / jnp.sum(jnp.exp(gl - gmax), axis=0, keepdims=True)
    el = jnp.zeros((e, tq), F32)
    for gi in range(g):
        el = jnp.where(gsel == gi, lt[g + gi * e:g + (gi + 1) * e], el)
    rowe = lax.broadcasted_iota(jnp.int32, el.shape, 0)
    v1 = jnp.max(el, axis=0, keepdims=True)
    i1 = jnp.min(jnp.where(el == v1, rowe, e), axis=0, keepdims=True)
    el2 = jnp.where(rowe == i1, -jnp.inf, el)
    v2 = jnp.max(el2, axis=0, keepdims=True)
    i2 = jnp.min(jnp.where(el2 == v2, rowe, e), axis=0, keepdims=True)
    e2 = jnp.exp(v2 - v1)
    den = 1.0 / (1.0 + e2)
    id1 = (gsel * e + i1).astype(F32)
    id2 = (gsel * e + i2).astype(F32)
    row8 = lax.broadcasted_iota(jnp.int32, (8, tq), 0)
    rt = jnp.where(row8 == 0, id1, jnp.where(row8 == 1, id2,
         jnp.where(row8 == 2, gw * den, jnp.where(row8 == 3, gw * e2 * den, 0.0))))
    rt_ref[0] = rt


def _cross_block(x, kv, g_cross, wq, g_qc, g_kc, wo, g_ffn, wr_hi, wr_lo, b_r):
    b, s, d = x.shape
    mlen = kv.shape[1]
    tq = min(CROSS_TQ, s)
    nq = s // tq
    dc = N_HEADS_C * DH_C
    nr = wr_hi.shape[0]
    assert s % tq == 0
    const = lambda bi, qi: (0, 0)
    return pl.pallas_call(
        _cross_kernel,
        grid=(b, nq),
        in_specs=[
            pl.BlockSpec((1, tq, d), lambda bi, qi: (bi, qi, 0)),
            pl.BlockSpec((1, d), const),
            pl.BlockSpec((d, dc), const),
            pl.BlockSpec((1, mlen, 2 * dc), lambda bi, qi: (bi, 0, 0)),
            pl.BlockSpec((1, DH_C), const),
            pl.BlockSpec((1, DH_C), const),
            pl.BlockSpec((dc, d), const),
            pl.BlockSpec((1, d), const),
            pl.BlockSpec((nr, d), const),
            pl.BlockSpec((nr, d), const),
            pl.BlockSpec((nr, 1), const),
        ],
        out_specs=[
            pl.BlockSpec((1, tq, d), lambda bi, qi: (bi, qi, 0)),
            pl.BlockSpec((tq * SUBLANES, LANES), lambda bi, qi: (bi * nq + qi, 0)),
            pl.BlockSpec((1, 8, tq), lambda bi, qi: (bi * nq + qi, 0, 0)),
        ],
        out_shape=[
            jax.ShapeDtypeStruct((b, s, d), F32),
            jax.ShapeDtypeStruct((b * s * SUBLANES, LANES), jnp.uint32),
            jax.ShapeDtypeStruct((b * nq, 8, tq), F32),
        ],
        input_output_aliases={0: 0},
        compiler_params=_cparams(("parallel", "parallel")),
        name="cross_block",
    )(x, g_cross.reshape(1, d), wq, kv, (g_qc * DH_C ** -0.5).reshape(1, DH_C), g_kc.reshape(1, DH_C),
      wo, g_ffn.reshape(1, d), wr_hi, wr_lo, b_r)


def _expert_kernel(idx_ref, be_ref, nr_ref, h_hbm, w1_ref, w3_ref, w2_ref, g_ref, z_hbm,
                   xbuf0, xbuf1, ybuf0, ybuf1, w1b, w3b, w2b, gsem, ssem, *, blk, n_steps):
    i = pl.program_id(0)
    n_cur = nr_ref[i]
    i_prev = jnp.maximum(i - 1, 0)
    n_prev = jnp.where(i >= 1, nr_ref[i_prev], 0)
    slot = i % 2
    tok_mask = (1 << TOK_BITS) - 1
    xbuf = (xbuf0, xbuf1)
    ybuf = (ybuf0, ybuf1)
    rs = SUBLANES

    def tile_rows(row):
        return pl.ds(pl.multiple_of(row * rs, rs), rs)

    def gather_copy(step, r, sl):
        row = idx_ref[step * blk + r] & tok_mask
        return pltpu.make_async_copy(h_hbm.at[tile_rows(row)], xbuf[sl].at[tile_rows(r)], gsem.at[sl])

    def scatter_row(step, r):
        return lax.shift_right_logical(idx_ref[step * blk + r], TOK_BITS)

    def scatter_copy(row, r, sl):
        return pltpu.make_async_copy(ybuf[sl].at[tile_rows(r)], z_hbm.at[tile_rows(row)], ssem.at[sl])

    def issue_gather(step, sl):
        def body(r, carry):
            gather_copy(step, r, sl).start()
            return carry
        lax.fori_loop(0, blk, body, 0, unroll=8)

    def wait_gather(sl):
        pltpu.make_async_copy(h_hbm.at[pl.ds(0, blk * rs)], xbuf[sl], gsem.at[sl]).wait()

    def issue_scatter(step, sl, n):
        def body(r, carry):
            row = scatter_row(step, r)

            @pl.when(r < n)
            def _():
                scatter_copy(row, r, sl).start()
            return carry
        lax.fori_loop(0, blk, body, 0, unroll=8)

    def wait_scatter(sl, n):
        k = blk
        while k >= 1:
            @pl.when((n & k) != 0)
            def _(k=k):
                pltpu.make_async_copy(ybuf[sl].at[pl.ds(0, k * rs)], z_hbm.at[pl.ds(0, k * rs)],
                                      ssem.at[sl]).wait()
            k //= 2

    @pl.when(i == 0)
    def _():
        issue_gather(0, 0)

    @pl.when((n_cur > 0) & ((i == 0) | (be_ref[i] != be_ref[jnp.maximum(i - 1, 0)])))
    def _():
        w1b[...] = w1_ref[0, 0].astype(BF16)
        w3b[...] = w3_ref[0, 0].astype(BF16)
        w2b[...] = w2_ref[0, 0].astype(BF16)

    def block_body(sl):
        for r in range(blk):
            gather_copy(i + 1, r, 1 - sl).start(priority=r % 2)
        for r in range(blk):
            row = scatter_row(i_prev, r)

            @pl.when(r < n_prev)
            def _(r=r, row=row):
                scatter_copy(row, r, 1 - sl).start(priority=r % 2)
        wait_gather(sl)
        xl, xr = _unpack_bf16_pair(_load_token_major(xbuf[sl], blk))
        xl, xr = xl.astype(BF16), xr.astype(BF16)
        dh = xl.shape[1]
        a1 = (jnp.dot(xl, w1b[0:dh, :], preferred_element_type=F32)
              + jnp.dot(xr, w1b[dh:, :], preferred_element_type=F32))
        a3 = (jnp.dot(xl, w3b[0:dh, :], preferred_element_type=F32)
              + jnp.dot(xr, w3b[dh:, :], preferred_element_type=F32))
        hid = (a1 * _sigmoid(a1) * a3 * g_ref[...]).astype(BF16)
        y = jnp.dot(hid, w2b[...], preferred_element_type=F32)
        _store_token_major(ybuf[sl], _pack_bf16_pair(y))
        wait_scatter(1 - sl, n_prev)

        @pl.when(i == n_steps - 1)
        def _():
            wait_gather(1 - sl)
            issue_scatter(i, sl, n_cur)
            wait_scatter(sl, n_cur)

    for sl in range(2):
        @pl.when((n_cur > 0) & (slot == sl))
        def _(sl=sl):
            block_body(sl)

    for sl in range(2):
        @pl.when((n_cur == 0) & (n_prev > 0) & (slot == sl))
        def _(sl=sl):
            wait_gather(sl)
            issue_scatter(i_prev, 1 - sl, n_prev)
            wait_scatter(1 - sl, n_prev)


def _expert_mlp(hpack, idx, blk_expert, nreal, gate_buf, w1, w3, w2, layer, z_rows):
    d = w1.shape[2]
    assert d == 2 * SUBLANES * LANES and hpack.shape[1] == LANES
    n_steps = idx.shape[0] // MOE_BLK - 1
    de = w1.shape[3]
    assert MOE_BLK & (MOE_BLK - 1) == 0
    return pl.pallas_call(
        functools.partial(_expert_kernel, blk=MOE_BLK, n_steps=n_steps),
        grid_spec=pltpu.PrefetchScalarGridSpec(
            num_scalar_prefetch=3,
            grid=(n_steps,),
            in_specs=[
                pl.BlockSpec(memory_space=pl.ANY),
                pl.BlockSpec((1, 1, d, de), lambda i, ix, be, nr: (layer, be[i], 0, 0)),
                pl.BlockSpec((1, 1, d, de), lambda i, ix, be, nr: (layer, be[i], 0, 0)),
                pl.BlockSpec((1, 1, de, d), lambda i, ix, be, nr: (layer, be[i], 0, 0)),
                pl.BlockSpec((MOE_BLK, 1), lambda i, ix, be, nr: (i, 0)),
            ],
            out_specs=pl.BlockSpec(memory_space=pl.ANY),
            scratch_shapes=[
                pltpu.VMEM((MOE_BLK * SUBLANES, LANES), jnp.uint32),
                pltpu.VMEM((MOE_BLK * SUBLANES, LANES), jnp.uint32),
                pltpu.VMEM((MOE_BLK * SUBLANES, LANES), jnp.uint32),
                pltpu.VMEM((MOE_BLK * SUBLANES, LANES), jnp.uint32),
                pltpu.VMEM((d, de), BF16),
                pltpu.VMEM((d, de), BF16),
                pltpu.VMEM((de, d), BF16),
                pltpu.SemaphoreType.DMA((2,)),
                pltpu.SemaphoreType.DMA((2,)),
            ],
        ),
        out_shape=jax.ShapeDtypeStruct((z_rows * SUBLANES, LANES), jnp.uint32),
        compiler_params=_cparams(("arbitrary",)),
        name="expert_mlp",
    )(idx, blk_expert, nreal, hpack, w1, w3, w2, gate_buf)


def _combine_kernel(x_ref, *refs):
    z_refs, o_ref = refs[:-1], refs[-1]
    tm, d = x_ref.shape
    acc_hi = x_ref[:, 0:d // 2]
    acc_lo = x_ref[:, d // 2:]
    for z_ref in z_refs:
        hi, lo = _unpack_bf16_pair(_load_token_major(z_ref, tm))
        acc_hi = acc_hi + hi
        acc_lo = acc_lo + lo
    o_ref[:, 0:d // 2] = acc_hi
    o_ref[:, d // 2:] = acc_lo


def _combine(x, z):
    t, d = x.shape
    tm = min(COMB_TM, t)
    assert t % tm == 0
    nt = t // tm
    return pl.pallas_call(
        _combine_kernel,
        grid=(nt,),
        in_specs=[pl.BlockSpec((tm, d), lambda i: (i, 0))] + [
            pl.BlockSpec((tm * SUBLANES, LANES), lambda i, k=k: (k * nt + i, 0)) for k in range(TOP_K)],
        out_specs=pl.BlockSpec((tm, d), lambda i: (i, 0)),
        out_shape=jax.ShapeDtypeStruct((t, d), F32),
        input_output_aliases={0: 0},
        compiler_params=_cparams(("parallel",)),
        name="moe_combine",
    )(x, *([z] * TOP_K))


def _t5_causal_bucket(dist):
    max_exact = N_BUCKETS // 2
    d_f = jnp.maximum(dist, 1).astype(F32)
    large = max_exact + (jnp.log(d_f / max_exact) / math.log(MAX_DISTANCE / max_exact)
                         * (N_BUCKETS - max_exact)).astype(jnp.int32)
    large = jnp.minimum(large, N_BUCKETS - 1)
    return jnp.where(dist < max_exact, dist, large)


def _bias_tiles(rel_bias_table, s, t):
    assert t >= MAX_DISTANCE
    rbd = rel_bias_table[_t5_causal_bucket(jnp.arange(s))].T.astype(F32)
    far = rel_bias_table[N_BUCKETS - 1].astype(F32)
    rbd = rbd - far[:, None]
    h = rbd.shape[0]

    def toeplitz(w):
        a = jnp.broadcast_to(w[:, None, :], (h, t, 2 * t)).reshape(h, 2 * t * t)
        return a[:, :t * (2 * t - 1)].reshape(h, t, 2 * t - 1)[:, :, :t]

    k = np.arange(2 * t)
    d_diag = np.where(k == 0, 0, np.minimum(2 * t - k, s - 1))
    w_diag = jnp.where((k >= 1) & (k <= t), NEG_INF, rbd[:, d_diag])
    d_off = np.minimum(np.where(k < t, t - k, 3 * t - k), s - 1)
    w_off = rbd[:, d_off]
    off = toeplitz(w_off).transpose(0, 2, 1) * LOG2E
    diag = jnp.maximum(toeplitz(w_diag).transpose(0, 2, 1) * LOG2E, NEG_INF)
    return jnp.concatenate([off, off], axis=2), jnp.concatenate([diag, diag], axis=2)


def _moe_plan(rt, t, n_experts, blk):
    nblk_rt, _, tq = rt.shape
    ids = rt[:, 0:TOP_K, :].astype(jnp.int32).transpose(0, 2, 1).reshape(t * TOP_K)
    gates = rt[:, TOP_K:2 * TOP_K, :].transpose(0, 2, 1).reshape(t * TOP_K)
    a = t * TOP_K
    order = jnp.argsort(ids).astype(jnp.int32)
    counts = jnp.sum((ids[None, :] == jnp.arange(n_experts)[:, None]).astype(jnp.int32), axis=1)
    start = jnp.cumsum(counts) - counts
    pcounts = (counts + blk - 1) // blk * blk
    pend = jnp.cumsum(pcounts)
    pstart = pend - pcounts
    n_blocks = -(-(a + n_experts * (blk - 1)) // blk)
    p = n_blocks * blk
    blk_row0 = jnp.arange(n_blocks, dtype=jnp.int32) * blk
    blk_expert = jnp.minimum(jnp.sum((pend[None, :] <= blk_row0[:, None]).astype(jnp.int32), axis=1),
                             n_experts - 1)
    nreal = jnp.clip((pstart + counts)[blk_expert] - blk_row0, 0, blk).astype(jnp.int32)
    within = (blk_row0 - pstart[blk_expert])[:, None] + jnp.arange(blk, dtype=jnp.int32)[None, :]
    valid = jnp.arange(blk, dtype=jnp.int32)[None, :] < nreal[:, None]
    src = jnp.where(valid, start[blk_expert][:, None] + within, 0).reshape(p)
    valid = valid.reshape(p)
    asg = order[src].astype(jnp.uint32)
    zrow = (asg % TOP_K) * t + asg // TOP_K
    idx = jnp.where(valid, (zrow << TOK_BITS) | (asg // TOP_K), 0).astype(jnp.uint32)
    idx = lax.bitcast_convert_type(idx, jnp.int32)
    idx = jnp.concatenate([idx, jnp.zeros((blk,), jnp.int32)])
    gate_buf = jnp.where(valid, gates[order[src]], 0.0)
    nreal = jnp.concatenate([nreal, jnp.zeros((1,), jnp.int32)])
    return idx, blk_expert.astype(jnp.int32), nreal, gate_buf.reshape(p, 1)


def kernel(x, mem, rel_bias_table, g_mix, w_in, g_q, g_k, diff_lambda, g_subln, conv_w, conv_b,
           conv_ln_g, conv_ln_b, w_out, g_cross, g_mem, wq_c, wkv_c, g_qc, g_kc, wo_c, g_ffn,
           w_group, b_group, w_router, b_router, w1, w3, w2):
    b, s, d = x.shape
    mlen = mem.shape[1]
    depth = w_in.shape[0]
    t = b * s
    d_conv = conv_w.shape[2]
    n_heads = (w_out.shape[1] - d_conv) // DV
    d_qk = n_heads * 2 * DQ
    n_experts = w_router.shape[2]
    assert t <= (1 << TOK_BITS) and TOP_K * t <= (1 << (32 - TOK_BITS))

    boff, bdiag = _bias_tiles(rel_bias_table, s, min(ATTN_T, s))
    bias_abs = jnp.maximum(jnp.max(jnp.abs(boff)),
                           jnp.max(jnp.where(bdiag > 0.5 * NEG_INF, jnp.abs(bdiag), 0.0)))
    n_router_rows = -(-(N_GROUPS + n_experts) // LANES) * LANES

    for l in range(depth):
        lam_init = 0.8 - 0.6 * math.exp(-0.3 * l)
        dl = diff_lambda[l].astype(F32)
        lam = jnp.exp(jnp.sum(dl[0] * dl[1])) - jnp.exp(jnp.sum(dl[2] * dl[3])) + lam_init
        gs = g_subln[l] * (1.0 - lam_init)
        qk_gain = jnp.concatenate([
            jnp.tile(g_q[l] * (DQ ** -0.5 * LOG2E), d_qk // DQ), jnp.tile(g_k[l], d_qk // DQ),
            jnp.ones((w_in.shape[2] - 2 * d_qk,), F32)]).reshape(1, -1)

        proj = _norm_matmul(x.reshape(t, d), g_mix[l], w_in[l].astype(BF16), qk_gain, 2 * d_qk)
        proj = proj.reshape(b, s, -1)
        score_bound = (jnp.max(jnp.abs(g_q[l])) * jnp.max(jnp.abs(g_k[l])) * (math.sqrt(DQ) * LOG2E * 1.02)
                       + bias_abs)
        attn = _diff_attention(proj, boff, bdiag, lam, gs, n_heads, score_bound)
        conv = _conformer_conv(proj, 2 * d_qk + n_heads * DV, conv_w[l], conv_b[l],
                               conv_ln_g[l], conv_ln_b[l])
        x = _outproj_residual(x.reshape(t, d), attn.reshape(t, -1), conv.reshape(t, -1),
                              w_out[l].astype(BF16), in_place=l > 0).reshape(b, s, d)

        kv = _norm_matmul(mem.reshape(b * mlen, d), g_mem[l], wkv_c[l].astype(BF16))
        wr = jnp.zeros((n_router_rows, d), F32)
        wr = wr.at[0:N_GROUPS].set(w_group[l].T).at[N_GROUPS:N_GROUPS + n_experts].set(w_router[l].T)
        wr_hi = wr.astype(BF16)
        wr_lo = (wr - wr_hi.astype(F32)).astype(BF16)
        b_r = jnp.zeros((n_router_rows, 1), F32)
        b_r = b_r.at[0:N_GROUPS, 0].set(b_group[l]).at[N_GROUPS:N_GROUPS + n_experts, 0].set(b_router[l])
        x, hpack, rt = _cross_block(x, kv.reshape(b, mlen, -1), g_cross[l], wq_c[l].astype(BF16),
                                    g_qc[l], g_kc[l], wo_c[l].astype(BF16), g_ffn[l], wr_hi, wr_lo, b_r)

        idx, blk_expert, nreal, gate_buf = _moe_plan(rt, t, n_experts, MOE_BLK)
        z = _expert_mlp(hpack, idx, blk_expert, nreal, gate_buf, w1, w3, w2, l, TOP_K * t)
        x = _combine(x.reshape(t, d), z).reshape(b, s, d)
    return x
```

```python
import functools
import math

import jax
import jax.numpy as jnp
import numpy as np
from jax import lax
from jax.experimental import pallas as pl
from jax.experimental.pallas import tpu as pltpu

DQ = 64
DV = 2 * DQ
CONV_WIDTH = 31
N_BUCKETS = 32
MAX_DISTANCE = 128
N_HEADS_C = 4
DH_C = 128
N_GROUPS = 8
EXPERTS_PER_GROUP = 8
TOP_K = 2
EPS = 1e-6
NEG_INF = -1e30
LOG2E = 1.4426950408889634
ATTN_SAFE_LOG2 = 96.0

LANES = 128
SUBLANES = 8
MXU_DIM = 256
VMEM_LIMIT_BYTES = 56 * 1024 * 1024

F32 = jnp.float32
BF16 = jnp.bfloat16

PROJ_TM = 1024
PROJ_TN = 512
ATTN_T = 256
CONV_TS = 256
CONV_HALO = 32
CONV_ROWS = 128
CROSS_TQ = 512
MOE_BLK = 256
COMB_TM = 256
TOK_BITS = 15
ROW_DMA_PRIORITY = 1


def _cparams(sem):
    return pltpu.CompilerParams(dimension_semantics=sem, vmem_limit_bytes=VMEM_LIMIT_BYTES)


def _sigmoid(x):
    return 1.0 / (1.0 + jnp.exp(-x))


def _store_token_major(ref, val):
    n = val.shape[0]
    for c in range(SUBLANES):
        ref[pl.ds(c, n, stride=SUBLANES), :] = val[:, c * LANES:(c + 1) * LANES]


def _load_token_major(ref, n):
    return jnp.concatenate([ref[pl.ds(c, n, stride=SUBLANES), :] for c in range(SUBLANES)], axis=1)


def _pack_bf16_pair(x):
    w = x.shape[1] // 2
    bits = lax.bitcast_convert_type(x.astype(BF16).astype(F32), jnp.uint32)
    return bits[:, 0:w] | (bits[:, w:] >> 16)


def _unpack_bf16_pair(word):
    hi = lax.bitcast_convert_type(word & jnp.uint32(0xFFFF0000), F32)
    lo = lax.bitcast_convert_type(word << 16, F32)
    return hi, lo


def _norm_matmul_kernel(x_ref, g_ref, w_ref, qkg_ref, seg_ref, o_ref, xn_ref, *, n_qk_tiles):
    j = pl.program_id(1)

    @pl.when(j == 0)
    def _():
        x = x_ref[...]
        inv = lax.rsqrt(jnp.mean(x * x, axis=-1, keepdims=True) + EPS)
        xn_ref[...] = (x * inv * g_ref[...]).astype(BF16)

    acc = jnp.dot(xn_ref[...], w_ref[...], preferred_element_type=F32)

    if n_qk_tiles == 0:
        o_ref[...] = acc.astype(o_ref.dtype)
        return

    @pl.when(j < n_qk_tiles)
    def _():
        sq = (acc * acc).astype(BF16)
        seg = seg_ref[...]
        parts = [jnp.dot(sq[:, c * MXU_DIM:(c + 1) * MXU_DIM], seg, preferred_element_type=F32)
                 for c in range(acc.shape[1] // MXU_DIM)]
        ss = jnp.concatenate(parts, axis=1)
        o_ref[...] = (acc * lax.rsqrt(ss * (1.0 / DQ) + EPS) * qkg_ref[...]).astype(o_ref.dtype)

    @pl.when(j >= n_qk_tiles)
    def _():
        o_ref[...] = acc.astype(o_ref.dtype)


def _norm_matmul(x, g, w, qk_gain=None, n_qk_cols=0):
    m, d = x.shape
    n = w.shape[1]
    tm = min(PROJ_TM, m)
    tn = min(PROJ_TN, n)
    assert m % tm == 0 and n % tn == 0 and n_qk_cols % tn == 0 and tn % MXU_DIM == 0
    if qk_gain is None:
        qk_gain = jnp.ones((1, n), F32)
    r = jnp.arange(MXU_DIM) // DQ
    seg = (r[:, None] == r[None, :]).astype(BF16)
    return pl.pallas_call(
        functools.partial(_norm_matmul_kernel, n_qk_tiles=n_qk_cols // tn),
        grid=(m // tm, n // tn),
        in_specs=[
            pl.BlockSpec((tm, d), lambda i, j: (i, 0)),
            pl.BlockSpec((1, d), lambda i, j: (0, 0)),
            pl.BlockSpec((d, tn), lambda i, j: (0, j)),
            pl.BlockSpec((1, tn), lambda i, j: (0, j)),
            pl.BlockSpec((MXU_DIM, MXU_DIM), lambda i, j: (0, 0)),
        ],
        out_specs=pl.BlockSpec((tm, tn), lambda i, j: (i, j)),
        out_shape=jax.ShapeDtypeStruct((m, n), BF16),
        scratch_shapes=[pltpu.VMEM((tm, d), BF16)],
        compiler_params=_cparams(("parallel", "arbitrary")),
        name="norm_matmul",
    )(x, g.reshape(1, d), w, qk_gain, seg)


def _attn_kernel(q_ref, k_ref, v_ref, boff_ref, bdiag_ref, lam_ref, gs_ref, o_ref,
                 qs_ref, vt_ref, m_ref, l_ref, acc_ref, *, t, bounded):
    i = pl.program_id(2)
    s_len = k_ref.shape[1]

    @pl.when(i == 0)
    def _():
        for c in range(s_len // t):
            vt_ref[c] = v_ref[0, c * t:(c + 1) * t, :].astype(F32).T.astype(BF16)

    q = q_ref[0].astype(F32)
    lane = lax.broadcasted_iota(jnp.int32, q.shape, 1)
    qs_ref[0:t, :] = jnp.where(lane < DQ, q, 0.0).astype(BF16)
    qs_ref[t:2 * t, :] = jnp.where(lane >= DQ, q, 0.0).astype(BF16)
    def scores(j):
        off = j * t if isinstance(j, int) else pl.multiple_of(j * t, t)
        k = k_ref[0, pl.ds(off, t), :]
        return lax.dot_general(k, qs_ref[...], (((1,), (1,)), ((), ())), preferred_element_type=F32)

    def finish(acc, l):
        ot = acc / l
        ot = ot[:, 0:t] - lam_ref[...] * ot[:, t:2 * t]
        inv = lax.rsqrt(jnp.mean(ot * ot, axis=0, keepdims=True) + EPS)
        o_ref[0] = (ot * inv * gs_ref[...]).T.astype(o_ref.dtype)

    if bounded:
        def sweep(ii):
            acc = l8 = None
            for j in range(ii + 1):
                st = scores(j)
                if j == ii:
                    st = st + bdiag_ref[0]
                elif j == ii - 1:
                    st = st + boff_ref[0]
                pt = jnp.exp2(st)
                lj = jnp.sum(pt.reshape(t // SUBLANES, SUBLANES, 2 * t), axis=0)
                pv = jnp.dot(vt_ref[j], pt.astype(BF16), preferred_element_type=F32)
                acc = pv if acc is None else acc + pv
                l8 = lj if l8 is None else l8 + lj
            finish(acc, jnp.sum(l8, axis=0, keepdims=True))

        for ii in range(s_len // t):
            pl.when(i == ii)(functools.partial(sweep, ii))
        return

    m_ref[...] = jnp.full(m_ref.shape, NEG_INF, F32)
    l_ref[...] = jnp.zeros(l_ref.shape, F32)
    acc_ref[...] = jnp.zeros(acc_ref.shape, F32)

    def softmax_pv(j, st):
        m_prev = m_ref[...]
        m_new = jnp.maximum(m_prev, jnp.max(st, axis=0, keepdims=True))
        alpha = jnp.exp2(m_prev - m_new)
        pt = jnp.exp2(st - m_new)
        l_ref[...] = alpha * l_ref[...] + jnp.sum(pt, axis=0, keepdims=True)
        pv = jnp.dot(vt_ref[j], pt.astype(BF16), preferred_element_type=F32)
        acc_ref[...] = alpha * acc_ref[...] + pv
        m_ref[...] = m_new

    def far_body(j, st):
        st_next = scores(j + 1)
        softmax_pv(j, st)
        return st_next

    def off_body(j, st):
        st_next = scores(j + 1)
        softmax_pv(j, st + boff_ref[0])
        return st_next

    st = lax.fori_loop(0, i - 1, far_body, scores(0))
    st = lax.fori_loop(jnp.maximum(i - 1, 0), i, off_body, st)
    softmax_pv(i, st + bdiag_ref[0])
    finish(acc_ref[...], l_ref[...])


def _diff_attention(proj, boff, bdiag, lam, gs, n_heads, score_bound):
    b, s, _ = proj.shape
    t = boff.shape[1]
    assert s % t == 0
    lam_row = jnp.full((1, t), lam, F32)
    gs_col = jnp.broadcast_to(gs.astype(F32)[:, None], (DV, t))
    operands = (proj, proj, proj, boff, bdiag, lam_row, gs_col)
    return lax.cond(score_bound <= ATTN_SAFE_LOG2,
                    functools.partial(_attn_call, t=t, n_heads=n_heads, bounded=True),
                    functools.partial(_attn_call, t=t, n_heads=n_heads, bounded=False),
                    *operands)


def _attn_call(*operands, t, n_heads, bounded):
    b, s, _ = operands[0].shape
    return pl.pallas_call(
        functools.partial(_attn_kernel, t=t, bounded=bounded),
        grid=(b, n_heads, s // t),
        in_specs=[
            pl.BlockSpec((1, t, DV), lambda bi, h, i: (bi, i, h)),
            pl.BlockSpec((1, s, DV), lambda bi, h, i: (bi, 0, n_heads + h)),
            pl.BlockSpec((1, s, DV), lambda bi, h, i: (bi, 0, 2 * n_heads + h)),
            pl.BlockSpec((1, t, 2 * t), lambda bi, h, i: (h, 0, 0)),
            pl.BlockSpec((1, t, 2 * t), lambda bi, h, i: (h, 0, 0)),
            pl.BlockSpec((1, t), lambda bi, h, i: (0, 0)),
            pl.BlockSpec((DV, t), lambda bi, h, i: (0, 0)),
        ],
        out_specs=pl.BlockSpec((1, t, DV), lambda bi, h, i: (bi, i, h)),
        out_shape=jax.ShapeDtypeStruct((b, s, n_heads * DV), BF16),
        scratch_shapes=[
            pltpu.VMEM((2 * t, DV), BF16),
            pltpu.VMEM((s // t, DV, t), BF16),
            pltpu.VMEM((1, 2 * t), F32),
            pltpu.VMEM((1, 2 * t), F32),
            pltpu.VMEM((DV, 2 * t), F32),
        ],
        compiler_params=_cparams(("parallel", "parallel", "arbitrary")),
        name="diff_attention_bounded" if bounded else "diff_attention",
    )(*operands)


def _conv_kernel(a_ref, gate_ref, w_ref, b_ref, lg_ref, lb_ref, o_ref, uext_ref, y_ref, win_ref, *, ts):
    si = pl.program_id(1)

    @pl.when(si == 0)
    def _():
        uext_ref[0:CONV_HALO, :] = jnp.zeros((CONV_HALO, uext_ref.shape[1]), F32)

    @pl.when(si > 0)
    def _():
        uext_ref[0:CONV_HALO, :] = uext_ref[ts:ts + CONV_HALO, :]

    a = a_ref[0].astype(F32)
    gate = gate_ref[0].astype(F32)
    uext_ref[CONV_HALO:CONV_HALO + ts, :] = a * _sigmoid(gate)

    base = CONV_HALO - (CONV_WIDTH - 1)
    rows = min(ts, CONV_ROWS)
    for c in range(uext_ref.shape[1] // LANES):
        cs = slice(c * LANES, (c + 1) * LANES)
        for r0 in range(0, ts, rows):
            acc = jnp.broadcast_to(b_ref[:, cs], (rows, LANES))
            for sh in range(SUBLANES):
                taps = [j for j in range(CONV_WIDTH) if (base + j) % SUBLANES == sh]
                span = max((base + j) // SUBLANES for j in taps) * SUBLANES + rows
                if sh:
                    win_ref[sh, 0:span, :] = uext_ref[r0 + sh:r0 + sh + span, cs]
                    win = win_ref[sh, 0:span, :]
                else:
                    win = uext_ref[r0:r0 + span, cs]
                for j in taps:
                    a0 = (base + j) // SUBLANES * SUBLANES
                    acc = acc + w_ref[j:j + 1, cs] * win[a0:a0 + rows]
            y_ref[r0:r0 + rows, cs] = acc

    y = y_ref[...]
    mu = jnp.mean(y, axis=-1, keepdims=True)
    yc = y - mu
    var = jnp.mean(yc * yc, axis=-1, keepdims=True)
    z = yc * lax.rsqrt(var + EPS) * lg_ref[...] + lb_ref[...]
    o_ref[0] = (z * _sigmoid(z)).astype(o_ref.dtype)


def _conformer_conv(proj, col0, conv_w, conv_b, ln_g, ln_b):
    b, s, _ = proj.shape
    c = conv_w.shape[1]
    ts = min(CONV_TS, s)
    assert s % ts == 0 and col0 % c == 0 and ts >= CONV_HALO
    return pl.pallas_call(
        functools.partial(_conv_kernel, ts=ts),
        grid=(b, s // ts),
        in_specs=[
            pl.BlockSpec((1, ts, c), lambda bi, si: (bi, si, col0 // c)),
            pl.BlockSpec((1, ts, c), lambda bi, si: (bi, si, col0 // c + 1)),
            pl.BlockSpec((CONV_WIDTH, c), lambda bi, si: (0, 0)),
            pl.BlockSpec((1, c), lambda bi, si: (0, 0)),
            pl.BlockSpec((1, c), lambda bi, si: (0, 0)),
            pl.BlockSpec((1, c), lambda bi, si: (0, 0)),
        ],
        out_specs=pl.BlockSpec((1, ts, c), lambda bi, si: (bi, si, 0)),
        out_shape=jax.ShapeDtypeStruct((b, s, c), BF16),
        scratch_shapes=[pltpu.VMEM((CONV_HALO + ts, c), F32), pltpu.VMEM((ts, c), F32),
                        pltpu.VMEM((SUBLANES, CONV_HALO + min(ts, CONV_ROWS), LANES), F32)],
        compiler_params=_cparams(("parallel", "arbitrary")),
        name="conformer_conv",
    )(proj, proj, conv_w, conv_b.reshape(1, c), ln_g.reshape(1, c), ln_b.reshape(1, c))


def _outproj_kernel(x_ref, a_ref, c_ref, w_ref, o_ref):
    ka = a_ref.shape[1]
    y = jnp.dot(a_ref[...], w_ref[0:ka, :], preferred_element_type=F32)
    y = y + jnp.dot(c_ref[...], w_ref[ka:, :], preferred_element_type=F32)
    o_ref[...] = x_ref[...] + y


def _outproj_residual(x, attn, conv, w, in_place):
    m, d = x.shape
    ka, kc = attn.shape[1], conv.shape[1]
    tm = min(PROJ_TM, m)
    tn = min(2 * PROJ_TN, d)
    assert m % tm == 0 and d % tn == 0
    return pl.pallas_call(
        _outproj_kernel,
        grid=(m // tm, d // tn),
        in_specs=[
            pl.BlockSpec((tm, tn), lambda i, j: (i, j)),
            pl.BlockSpec((tm, ka), lambda i, j: (i, 0)),
            pl.BlockSpec((tm, kc), lambda i, j: (i, 0)),
            pl.BlockSpec((ka + kc, tn), lambda i, j: (0, j)),
        ],
        out_specs=pl.BlockSpec((tm, tn), lambda i, j: (i, j)),
        out_shape=jax.ShapeDtypeStruct((m, d), F32),
        input_output_aliases={0: 0} if in_place else {},
        compiler_params=_cparams(("parallel", "parallel")),
        name="outproj_residual",
    )(x, attn, conv, w)


def _cross_kernel(x_ref, gc_ref, wq_ref, kv_ref, gq_ref, gk_ref, wo_ref, gf_ref,
                  wrh_ref, wrl_ref, br_ref, xo_ref, hp_ref, rt_ref):
    x = x_ref[0]
    d = x.shape[1]
    tq = x.shape[0]
    h = (x * lax.rsqrt(jnp.mean(x * x, axis=-1, keepdims=True) + EPS) * gc_ref[...]).astype(BF16)
    q = jnp.dot(h, wq_ref[...], preferred_element_type=F32)
    dc = N_HEADS_C * DH_C
    outs = []
    for hd in range(N_HEADS_C):
        cs = slice(hd * DH_C, (hd + 1) * DH_C)
        qh = q[:, cs]
        qh = qh * lax.rsqrt(jnp.mean(qh * qh, axis=-1, keepdims=True) + EPS) * gq_ref[...]
        kh = kv_ref[0, :, cs].astype(F32)
        kh = kh * lax.rsqrt(jnp.mean(kh * kh, axis=-1, keepdims=True) + EPS) * gk_ref[...]
        vh = kv_ref[0, :, dc + hd * DH_C:dc + (hd + 1) * DH_C]
        s = lax.dot_general(qh.astype(BF16), kh.astype(BF16), (((1,), (1,)), ((), ())),
                            preferred_element_type=F32)
        p = jnp.exp(s - jnp.max(s, axis=-1, keepdims=True))
        p = p / jnp.sum(p, axis=-1, keepdims=True)
        outs.append(jnp.dot(p.astype(BF16), vh, preferred_element_type=F32))
    o = jnp.concatenate(outs, axis=1).astype(BF16)
    x2 = x + jnp.dot(o, wo_ref[...], preferred_element_type=F32)
    xo_ref[0] = x2

    h2 = x2 * lax.rsqrt(jnp.mean(x2 * x2, axis=-1, keepdims=True) + EPS) * gf_ref[...]
    h2_hi = h2.astype(BF16)
    h2_hi32 = h2_hi.astype(F32)
    h2_lo = (h2 - h2_hi32).astype(BF16)

    _store_token_major(hp_ref, _pack_bf16_pair(h2_hi32))

    nt = (((1,), (1,)), ((), ()))
    lt = (lax.dot_general(wrh_ref[...], h2_hi, nt, preferred_element_type=F32)
          + lax.dot_general(wrh_ref[...], h2_lo, nt, preferred_element_type=F32)
          + lax.dot_general(wrl_ref[...], h2_hi, nt, preferred_element_type=F32)) + br_ref[...]
    g, e = N_GROUPS, EXPERTS_PER_GROUP
    gl = lt[0:g]
    rowg = lax.broadcasted_iota(jnp.int32, gl.shape, 0)
    gmax = jnp.max(gl, axis=0, keepdims=True)
    gsel = jnp.min(jnp.where(gl == gmax, rowg, g), axis=0, keepdims=True)
    gw = 1.0 / jnp.sum(jnp.exp(gl - gmax), axis=0, keepdims=True)
    el = jnp.zeros((e, tq), F32)
    for gi in range(g):
        el = jnp.where(gsel == gi, lt[g + gi * e:g + (gi + 1) * e], el)
    rowe = lax.broadcasted_iota(jnp.int32, el.shape, 0)
    v1 = jnp.max(el, axis=0, keepdims=True)
    i1 = jnp.min(jnp.where(el == v1, rowe, e), axis=0, keepdims=True)
    el2 = jnp.where(rowe == i1, -jnp.inf, el)
    v2 = jnp.max(el2, axis=0, keepdims=True)
    i2 = jnp.min(jnp.where(el2 == v2, rowe, e), axis=0, keepdims=True)
    e2 = jnp.exp(v2 - v1)
    den = 1.0 / (1.0 + e2)
    id1 = (gsel * e + i1).astype(F32)
    id2 = (gsel * e + i2).astype(F32)
    row8 = lax.broadcasted_iota(jnp.int32, (8, tq), 0)
    rt = jnp.where(row8 == 0, id1, jnp.where(row8 == 1, id2,
         jnp.where(row8 == 2, gw * den, jnp.where(row8 == 3, gw * e2 * den, 0.0))))
    rt_ref[0] = rt


def _cross_block(x, kv, g_cross, wq, g_qc, g_kc, wo, g_ffn, wr_hi, wr_lo, b_r):
    b, s, d = x.shape
    mlen = kv.shape[1]
    tq = min(CROSS_TQ, s)
    nq = s // tq
    dc = N_HEADS_C * DH_C
    nr = wr_hi.shape[0]
    assert s % tq == 0
    const = lambda bi, qi: (0, 0)
    return pl.pallas_call(
        _cross_kernel,
        grid=(b, nq),
        in_specs=[
            pl.BlockSpec((1, tq, d), lambda bi, qi: (bi, qi, 0)),
            pl.BlockSpec((1, d), const),
            pl.BlockSpec((d, dc), const),
            pl.BlockSpec((1, mlen, 2 * dc), lambda bi, qi: (bi, 0, 0)),
            pl.BlockSpec((1, DH_C), const),
            pl.BlockSpec((1, DH_C), const),
            pl.BlockSpec((dc, d), const),
            pl.BlockSpec((1, d), const),
            pl.BlockSpec((nr, d), const),
            pl.BlockSpec((nr, d), const),
            pl.BlockSpec((nr, 1), const),
        ],
        out_specs=[
            pl.BlockSpec((1, tq, d), lambda bi, qi: (bi, qi, 0)),
            pl.BlockSpec((tq * SUBLANES, LANES), lambda bi, qi: (bi * nq + qi, 0)),
            pl.BlockSpec((1, 8, tq), lambda bi, qi: (bi * nq + qi, 0, 0)),
        ],
        out_shape=[
            jax.ShapeDtypeStruct((b, s, d), F32),
            jax.ShapeDtypeStruct((b * s * SUBLANES, LANES), jnp.uint32),
            jax.ShapeDtypeStruct((b * nq, 8, tq), F32),
        ],
        input_output_aliases={0: 0},
        compiler_params=_cparams(("parallel", "parallel")),
        name="cross_block",
    )(x, g_cross.reshape(1, d), wq, kv, (g_qc * DH_C ** -0.5).reshape(1, DH_C), g_kc.reshape(1, DH_C),
      wo, g_ffn.reshape(1, d), wr_hi, wr_lo, b_r)


def _expert_kernel(idx_ref, be_ref, nr_ref, h_hbm, w1_ref, w3_ref, w2_ref, g_ref, z_hbm,
                   xbuf0, xbuf1, ybuf0, ybuf1, w1b, w3b, w2b, gsem, ssem, *, blk, n_steps):
    i = pl.program_id(0)
    n_cur = nr_ref[i]
    i_prev = jnp.maximum(i - 1, 0)
    n_prev = jnp.where(i >= 1, nr_ref[i_prev], 0)
    slot = i % 2
    tok_mask = (1 << TOK_BITS) - 1
    xbuf = (xbuf0, xbuf1)
    ybuf = (ybuf0, ybuf1)
    rs = SUBLANES

    def tile_rows(row):
        return pl.ds(pl.multiple_of(row * rs, rs), rs)

    def gather_copy(step, r, sl):
        row = idx_ref[step * blk + r] & tok_mask
        return pltpu.make_async_copy(h_hbm.at[tile_rows(row)], xbuf[sl].at[tile_rows(r)], gsem.at[sl])

    def scatter_row(step, r):
        return lax.shift_right_logical(idx_ref[step * blk + r], TOK_BITS)

    def scatter_copy(row, r, sl):
        return pltpu.make_async_copy(ybuf[sl].at[tile_rows(r)], z_hbm.at[tile_rows(row)], ssem.at[sl])

    def issue_gather(step, sl):
        def body(r, carry):
            gather_copy(step, r, sl).start()
            return carry
        lax.fori_loop(0, blk, body, 0, unroll=8)

    def wait_gather(sl):
        pltpu.make_async_copy(h_hbm.at[pl.ds(0, blk * rs)], xbuf[sl], gsem.at[sl]).wait()

    def issue_scatter(step, sl, n):
        def body(r, carry):
            row = scatter_row(step, r)

            @pl.when(r < n)
            def _():
                scatter_copy(row, r, sl).start()
            return carry
        lax.fori_loop(0, blk, body, 0, unroll=8)

    def wait_scatter(sl, n):
        k = blk
        while k >= 1:
            @pl.when((n & k) != 0)
            def _(k=k):
                pltpu.make_async_copy(ybuf[sl].at[pl.ds(0, k * rs)], z_hbm.at[pl.ds(0, k * rs)],
                                      ssem.at[sl]).wait()
            k //= 2

    @pl.when(i == 0)
    def _():
        issue_gather(0, 0)

    @pl.when((n_cur > 0) & ((i == 0) | (be_ref[i] != be_ref[jnp.maximum(i - 1, 0)])))
    def _():
        w1b[...] = w1_ref[0, 0].astype(BF16)
        w3b[...] = w3_ref[0, 0].astype(BF16)
        w2b[...] = w2_ref[0, 0].astype(BF16)

    def block_body(sl):
        for r in range(blk):
            gather_copy(i + 1, r, 1 - sl).start(priority=ROW_DMA_PRIORITY)
        for r in range(blk):
            row = scatter_row(i_prev, r)

            @pl.when(r < n_prev)
            def _(r=r, row=row):
                scatter_copy(row, r, 1 - sl).start(priority=ROW_DMA_PRIORITY)
        wait_gather(sl)
        xl, xr = _unpack_bf16_pair(_load_token_major(xbuf[sl], blk))
        xl, xr = xl.astype(BF16), xr.astype(BF16)
        dh = xl.shape[1]
        a1 = (jnp.dot(xl, w1b[0:dh, :], preferred_element_type=F32)
              + jnp.dot(xr, w1b[dh:, :], preferred_element_type=F32))
        a3 = (jnp.dot(xl, w3b[0:dh, :], preferred_element_type=F32)
              + jnp.dot(xr, w3b[dh:, :], preferred_element_type=F32))
        hid = (a1 * _sigmoid(a1) * a3 * g_ref[...]).astype(BF16)
        y = jnp.dot(hid, w2b[...], preferred_element_type=F32)
        _store_token_major(ybuf[sl], _pack_bf16_pair(y))
        wait_scatter(1 - sl, n_prev)

        @pl.when(i == n_steps - 1)
        def _():
            wait_gather(1 - sl)
            issue_scatter(i, sl, n_cur)
            wait_scatter(sl, n_cur)

    for sl in range(2):
        @pl.when((n_cur > 0) & (slot == sl))
        def _(sl=sl):
            block_body(sl)

    for sl in range(2):
        @pl.when((n_cur == 0) & (n_prev > 0) & (slot == sl))
        def _(sl=sl):
            wait_gather(sl)
            issue_scatter(i_prev, 1 - sl, n_prev)
            wait_scatter(1 - sl, n_prev)


def _expert_mlp(hpack, idx, blk_expert, nreal, gate_buf, w1, w3, w2, layer, z_rows):
    d = w1.shape[2]
    assert d == 2 * SUBLANES * LANES and hpack.shape[1] == LANES
    n_steps = idx.shape[0] // MOE_BLK - 1
    de = w1.shape[3]
    assert MOE_BLK & (MOE_BLK - 1) == 0
    return pl.pallas_call(
        functools.partial(_expert_kernel, blk=MOE_BLK, n_steps=n_steps),
        grid_spec=pltpu.PrefetchScalarGridSpec(
            num_scalar_prefetch=3,
            grid=(n_steps,),
            in_specs=[
                pl.BlockSpec(memory_space=pl.ANY),
                pl.BlockSpec((1, 1, d, de), lambda i, ix, be, nr: (layer, be[i], 0, 0)),
                pl.BlockSpec((1, 1, d, de), lambda i, ix, be, nr: (layer, be[i], 0, 0)),
                pl.BlockSpec((1, 1, de, d), lambda i, ix, be, nr: (layer, be[i], 0, 0)),
                pl.BlockSpec((MOE_BLK, 1), lambda i, ix, be, nr: (i, 0)),
            ],
            out_specs=pl.BlockSpec(memory_space=pl.ANY),
            scratch_shapes=[
                pltpu.VMEM((MOE_BLK * SUBLANES, LANES), jnp.uint32),
                pltpu.VMEM((MOE_BLK * SUBLANES, LANES), jnp.uint32),
                pltpu.VMEM((MOE_BLK * SUBLANES, LANES), jnp.uint32),
                pltpu.VMEM((MOE_BLK * SUBLANES, LANES), jnp.uint32),
                pltpu.VMEM((d, de), BF16),
                pltpu.VMEM((d, de), BF16),
                pltpu.VMEM((de, d), BF16),
                pltpu.SemaphoreType.DMA((2,)),
                pltpu.SemaphoreType.DMA((2,)),
            ],
        ),
        out_shape=jax.ShapeDtypeStruct((z_rows * SUBLANES, LANES), jnp.uint32),
        compiler_params=_cparams(("arbitrary",)),
        name="expert_mlp",
    )(idx, blk_expert, nreal, hpack, w1, w3, w2, gate_buf)


def _combine_kernel(x_ref, *refs):
    z_refs, o_ref = refs[:-1], refs[-1]
    tm, d = x_ref.shape
    acc_hi = x_ref[:, 0:d // 2]
    acc_lo = x_ref[:, d // 2:]
    for z_ref in z_refs:
        hi, lo = _unpack_bf16_pair(_load_token_major(z_ref, tm))
        acc_hi = acc_hi + hi
        acc_lo = acc_lo + lo
    o_ref[:, 0:d // 2] = acc_hi
    o_ref[:, d // 2:] = acc_lo


def _combine(x, z):
    t, d = x.shape
    tm = min(COMB_TM, t)
    assert t % tm == 0
    nt = t // tm
    return pl.pallas_call(
        _combine_kernel,
        grid=(nt,),
        in_specs=[pl.BlockSpec((tm, d), lambda i: (i, 0))] + [
            pl.BlockSpec((tm * SUBLANES, LANES), lambda i, k=k: (k * nt + i, 0)) for k in range(TOP_K)],
        out_specs=pl.BlockSpec((tm, d), lambda i: (i, 0)),
        out_shape=jax.ShapeDtypeStruct((t, d), F32),
        input_output_aliases={0: 0},
        compiler_params=_cparams(("parallel",)),
        name="moe_combine",
    )(x, *([z] * TOP_K))


def _t5_causal_bucket(dist):
    max_exact = N_BUCKETS // 2
    d_f = jnp.maximum(dist, 1).astype(F32)
    large = max_exact + (jnp.log(d_f / max_exact) / math.log(MAX_DISTANCE / max_exact)
                         * (N_BUCKETS - max_exact)).astype(jnp.int32)
    large = jnp.minimum(large, N_BUCKETS - 1)
    return jnp.where(dist < max_exact, dist, large)


def _bias_tiles(rel_bias_table, s, t):
    assert t >= MAX_DISTANCE
    rbd = rel_bias_table[_t5_causal_bucket(jnp.arange(s))].T.astype(F32)
    far = rel_bias_table[N_BUCKETS - 1].astype(F32)
    rbd = rbd - far[:, None]
    h = rbd.shape[0]

    def toeplitz(w):
        a = jnp.broadcast_to(w[:, None, :], (h, t, 2 * t)).reshape(h, 2 * t * t)
        return a[:, :t * (2 * t - 1)].reshape(h, t, 2 * t - 1)[:, :, :t]

    k = np.arange(2 * t)
    d_diag = np.where(k == 0, 0, np.minimum(2 * t - k, s - 1))
    w_diag = jnp.where((k >= 1) & (k <= t), NEG_INF, rbd[:, d_diag])
    d_off = np.minimum(np.where(k < t, t - k, 3 * t - k), s - 1)
    w_off = rbd[:, d_off]
    off = toeplitz(w_off).transpose(0, 2, 1) * LOG2E
    diag = jnp.maximum(toeplitz(w_diag).transpose(0, 2, 1) * LOG2E, NEG_INF)
    return jnp.concatenate([off, off], axis=2), jnp.concatenate([diag, diag], axis=2)


def _moe_plan(rt, t, n_experts, blk):
    nblk_rt, _, tq = rt.shape
    ids = rt[:, 0:TOP_K, :].astype(jnp.int32).transpose(0, 2, 1).reshape(t * TOP_K)
    gates = rt[:, TOP_K:2 * TOP_K, :].transpose(0, 2, 1).reshape(t * TOP_K)
    a = t * TOP_K
    order = jnp.argsort(ids).astype(jnp.int32)
    counts = jnp.sum((ids[None, :] == jnp.arange(n_experts)[:, None]).astype(jnp.int32), axis=1)
    start = jnp.cumsum(counts) - counts
    pcounts = (counts + blk - 1) // blk * blk
    pend = jnp.cumsum(pcounts)
    pstart = pend - pcounts
    n_blocks = -(-(a + n_experts * (blk - 1)) // blk)
    p = n_blocks * blk
    blk_row0 = jnp.arange(n_blocks, dtype=jnp.int32) * blk
    blk_expert = jnp.minimum(jnp.sum((pend[None, :] <= blk_row0[:, None]).astype(jnp.int32), axis=1),
                             n_experts - 1)
    nreal = jnp.clip((pstart + counts)[blk_expert] - blk_row0, 0, blk).astype(jnp.int32)
    within = (blk_row0 - pstart[blk_expert])[:, None] + jnp.arange(blk, dtype=jnp.int32)[None, :]
    valid = jnp.arange(blk, dtype=jnp.int32)[None, :] < nreal[:, None]
    src = jnp.where(valid, start[blk_expert][:, None] + within, 0).reshape(p)
    valid = valid.reshape(p)
    asg = order[src].astype(jnp.uint32)
    zrow = (asg % TOP_K) * t + asg // TOP_K
    idx = jnp.where(valid, (zrow << TOK_BITS) | (asg // TOP_K), 0).astype(jnp.uint32)
    idx = lax.bitcast_convert_type(idx, jnp.int32)
    idx = jnp.concatenate([idx, jnp.zeros((blk,), jnp.int32)])
    gate_buf = jnp.where(valid, gates[order[src]], 0.0)
    nreal = jnp.concatenate([nreal, jnp.zeros((1,), jnp.int32)])
    return idx, blk_expert.astype(jnp.int32), nreal, gate_buf.reshape(p, 1)


def kernel(x, mem, rel_bias_table, g_mix, w_in, g_q, g_k, diff_lambda, g_subln, conv_w, conv_b,
           conv_ln_g, conv_ln_b, w_out, g_cross, g_mem, wq_c, wkv_c, g_qc, g_kc, wo_c, g_ffn,
           w_group, b_group, w_router, b_router, w1, w3, w2):
    b, s, d = x.shape
    mlen = mem.shape[1]
    depth = w_in.shape[0]
    t = b * s
    d_conv = conv_w.shape[2]
    n_heads = (w_out.shape[1] - d_conv) // DV
    d_qk = n_heads * 2 * DQ
    n_experts = w_router.shape[2]
    assert t <= (1 << TOK_BITS) and TOP_K * t <= (1 << (32 - TOK_BITS))

    boff, bdiag = _bias_tiles(rel_bias_table, s, min(ATTN_T, s))
    bias_abs = jnp.maximum(jnp.max(jnp.abs(boff)),
                           jnp.max(jnp.where(bdiag > 0.5 * NEG_INF, jnp.abs(bdiag), 0.0)))
    n_router_rows = -(-(N_GROUPS + n_experts) // LANES) * LANES

    for l in range(depth):
        lam_init = 0.8 - 0.6 * math.exp(-0.3 * l)
        dl = diff_lambda[l].astype(F32)
        lam = jnp.exp(jnp.sum(dl[0] * dl[1])) - jnp.exp(jnp.sum(dl[2] * dl[3])) + lam_init
        gs = g_subln[l] * (1.0 - lam_init)
        qk_gain = jnp.concatenate([
            jnp.tile(g_q[l] * (DQ ** -0.5 * LOG2E), d_qk // DQ), jnp.tile(g_k[l], d_qk // DQ),
            jnp.ones((w_in.shape[2] - 2 * d_qk,), F32)]).reshape(1, -1)

        proj = _norm_matmul(x.reshape(t, d), g_mix[l], w_in[l].astype(BF16), qk_gain, 2 * d_qk)
        proj = proj.reshape(b, s, -1)
        score_bound = (jnp.max(jnp.abs(g_q[l])) * jnp.max(jnp.abs(g_k[l])) * (math.sqrt(DQ) * LOG2E * 1.02)
                       + bias_abs)
        attn = _diff_attention(proj, boff, bdiag, lam, gs, n_heads, score_bound)
        conv = _conformer_conv(proj, 2 * d_qk + n_heads * DV, conv_w[l], conv_b[l],
                               conv_ln_g[l], conv_ln_b[l])
        x = _outproj_residual(x.reshape(t, d), attn.reshape(t, -1), conv.reshape(t, -1),
                              w_out[l].astype(BF16), in_place=l > 0).reshape(b, s, d)

        kv = _norm_matmul(mem.reshape(b * mlen, d), g_mem[l], wkv_c[l].astype(BF16))
        wr = jnp.zeros((n_router_rows, d), F32)
        wr = wr.at[0:N_GROUPS].set(w_group[l].T).at[N_GROUPS:N_GROUPS + n_experts].set(w_router[l].T)
        wr_hi = wr.astype(BF16)
        wr_lo = (wr - wr_hi.astype(F32)).astype(BF16)
        b_r = jnp.zeros((n_router_rows, 1), F32)
        b_r = b_r.at[0:N_GROUPS, 0].set(b_group[l]).at[N_GROUPS:N_GROUPS + n_experts, 0].set(b_router[l])
        x, hpack, rt = _cross_block(x, kv.reshape(b, mlen, -1), g_cross[l], wq_c[l].astype(BF16),
                                    g_qc[l], g_kc[l], wo_c[l].astype(BF16), g_ffn[l], wr_hi, wr_lo, b_r)

        idx, blk_expert, nreal, gate_buf = _moe_plan(rt, t, n_experts, MOE_BLK)
        z = _expert_mlp(hpack, idx, blk_expert, nreal, gate_buf, w1, w3, w2, l, TOP_K * t)
        x = _combine(x.reshape(t, d), z).reshape(b, s, d)
    return x
```

```python
import functools
import math

import jax
import jax.numpy as jnp
import numpy as np
from jax import lax
from jax.experimental import pallas as pl
from jax.experimental.pallas import tpu as pltpu

DQ = 64
DV = 2 * DQ
CONV_WIDTH = 31
N_BUCKETS = 32
MAX_DISTANCE = 128
N_HEADS_C = 4
DH_C = 128
N_GROUPS = 8
EXPERTS_PER_GROUP = 8
TOP_K = 2
EPS = 1e-6
NEG_INF = -1e30
LOG2E = 1.4426950408889634
ATTN_SAFE_LOG2 = 96.0

LANES = 128
SUBLANES = 8
MXU_DIM = 256
VMEM_LIMIT_BYTES = 56 * 1024 * 1024

F32 = jnp.float32
BF16 = jnp.bfloat16

PROJ_TM = 1024
PROJ_TN = 512
ATTN_T = 256
CONV_TS = 256
CONV_HALO = 32
CONV_ROWS = 128
CROSS_TQ = 512
MOE_BLK = 256
COMB_TM = 256
TOK_BITS = 15
ROW_DMA_PRIORITY = 1
MOE_BUFS = 3


def _cparams(sem):
    return pltpu.CompilerParams(dimension_semantics=sem, vmem_limit_bytes=VMEM_LIMIT_BYTES)


def _sigmoid(x):
    return 1.0 / (1.0 + jnp.exp(-x))


def _store_token_major(ref, val):
    n = val.shape[0]
    for c in range(SUBLANES):
        ref[pl.ds(c, n, stride=SUBLANES), :] = val[:, c * LANES:(c + 1) * LANES]


def _load_token_major(ref, n):
    return jnp.concatenate([ref[pl.ds(c, n, stride=SUBLANES), :] for c in range(SUBLANES)], axis=1)


def _pack_bf16_pair(x):
    w = x.shape[1] // 2
    bits = lax.bitcast_convert_type(x.astype(BF16).astype(F32), jnp.uint32)
    return bits[:, 0:w] | (bits[:, w:] >> 16)


def _unpack_bf16_pair(word):
    hi = lax.bitcast_convert_type(word & jnp.uint32(0xFFFF0000), F32)
    lo = lax.bitcast_convert_type(word << 16, F32)
    return hi, lo


def _norm_matmul_kernel(x_ref, g_ref, w_ref, qkg_ref, seg_ref, o_ref, xn_ref, *, n_qk_tiles):
    j = pl.program_id(1)

    @pl.when(j == 0)
    def _():
        x = x_ref[...]
        inv = lax.rsqrt(jnp.mean(x * x, axis=-1, keepdims=True) + EPS)
        xn_ref[...] = (x * inv * g_ref[...]).astype(BF16)

    acc = jnp.dot(xn_ref[...], w_ref[...], preferred_element_type=F32)

    if n_qk_tiles == 0:
        o_ref[...] = acc.astype(o_ref.dtype)
        return

    @pl.when(j < n_qk_tiles)
    def _():
        sq = (acc * acc).astype(BF16)
        seg = seg_ref[...]
        parts = [jnp.dot(sq[:, c * MXU_DIM:(c + 1) * MXU_DIM], seg, preferred_element_type=F32)
                 for c in range(acc.shape[1] // MXU_DIM)]
        ss = jnp.concatenate(parts, axis=1)
        o_ref[...] = (acc * lax.rsqrt(ss * (1.0 / DQ) + EPS) * qkg_ref[...]).astype(o_ref.dtype)

    @pl.when(j >= n_qk_tiles)
    def _():
        o_ref[...] = acc.astype(o_ref.dtype)


def _norm_matmul(x, g, w, qk_gain=None, n_qk_cols=0):
    m, d = x.shape
    n = w.shape[1]
    tm = min(PROJ_TM, m)
    tn = min(PROJ_TN, n)
    assert m % tm == 0 and n % tn == 0 and n_qk_cols % tn == 0 and tn % MXU_DIM == 0
    if qk_gain is None:
        qk_gain = jnp.ones((1, n), F32)
    r = jnp.arange(MXU_DIM) // DQ
    seg = (r[:, None] == r[None, :]).astype(BF16)
    return pl.pallas_call(
        functools.partial(_norm_matmul_kernel, n_qk_tiles=n_qk_cols // tn),
        grid=(m // tm, n // tn),
        in_specs=[
            pl.BlockSpec((tm, d), lambda i, j: (i, 0)),
            pl.BlockSpec((1, d), lambda i, j: (0, 0)),
            pl.BlockSpec((d, tn), lambda i, j: (0, j)),
            pl.BlockSpec((1, tn), lambda i, j: (0, j)),
            pl.BlockSpec((MXU_DIM, MXU_DIM), lambda i, j: (0, 0)),
        ],
        out_specs=pl.BlockSpec((tm, tn), lambda i, j: (i, j)),
        out_shape=jax.ShapeDtypeStruct((m, n), BF16),
        scratch_shapes=[pltpu.VMEM((tm, d), BF16)],
        compiler_params=_cparams(("parallel", "arbitrary")),
        name="norm_matmul",
    )(x, g.reshape(1, d), w, qk_gain, seg)


def _attn_kernel(q_ref, k_ref, v_ref, boff_ref, bdiag_ref, lam_ref, gs_ref, o_ref,
                 qs_ref, vt_ref, m_ref, l_ref, acc_ref, *, t, bounded):
    i = pl.program_id(2)
    s_len = k_ref.shape[1]

    @pl.when(i == 0)
    def _():
        for c in range(s_len // t):
            vt_ref[c] = v_ref[0, c * t:(c + 1) * t, :].astype(F32).T.astype(BF16)

    q = q_ref[0].astype(F32)
    lane = lax.broadcasted_iota(jnp.int32, q.shape, 1)
    qs_ref[0:t, :] = jnp.where(lane < DQ, q, 0.0).astype(BF16)
    qs_ref[t:2 * t, :] = jnp.where(lane >= DQ, q, 0.0).astype(BF16)
    def scores(j):
        off = j * t if isinstance(j, int) else pl.multiple_of(j * t, t)
        k = k_ref[0, pl.ds(off, t), :]
        return lax.dot_general(k, qs_ref[...], (((1,), (1,)), ((), ())), preferred_element_type=F32)

    def finish(acc, l):
        ot = acc / l
        ot = ot[:, 0:t] - lam_ref[...] * ot[:, t:2 * t]
        inv = lax.rsqrt(jnp.mean(ot * ot, axis=0, keepdims=True) + EPS)
        o_ref[0] = (ot * inv * gs_ref[...]).T.astype(o_ref.dtype)

    if bounded:
        def sweep(ii):
            acc = l8 = None
            for j in range(ii + 1):
                st = scores(j)
                if j == ii:
                    st = st + bdiag_ref[0]
                elif j == ii - 1:
                    st = st + boff_ref[0]
                pt = jnp.exp2(st)
                lj = jnp.sum(pt.reshape(t // SUBLANES, SUBLANES, 2 * t), axis=0)
                pv = jnp.dot(vt_ref[j], pt.astype(BF16), preferred_element_type=F32)
                acc = pv if acc is None else acc + pv
                l8 = lj if l8 is None else l8 + lj
            finish(acc, jnp.sum(l8, axis=0, keepdims=True))

        for ii in range(s_len // t):
            pl.when(i == ii)(functools.partial(sweep, ii))
        return

    m_ref[...] = jnp.full(m_ref.shape, NEG_INF, F32)
    l_ref[...] = jnp.zeros(l_ref.shape, F32)
    acc_ref[...] = jnp.zeros(acc_ref.shape, F32)

    def softmax_pv(j, st):
        m_prev = m_ref[...]
        m_new = jnp.maximum(m_prev, jnp.max(st, axis=0, keepdims=True))
        alpha = jnp.exp2(m_prev - m_new)
        pt = jnp.exp2(st - m_new)
        l_ref[...] = alpha * l_ref[...] + jnp.sum(pt, axis=0, keepdims=True)
        pv = jnp.dot(vt_ref[j], pt.astype(BF16), preferred_element_type=F32)
        acc_ref[...] = alpha * acc_ref[...] + pv
        m_ref[...] = m_new

    def far_body(j, st):
        st_next = scores(j + 1)
        softmax_pv(j, st)
        return st_next

    def off_body(j, st):
        st_next = scores(j + 1)
        softmax_pv(j, st + boff_ref[0])
        return st_next

    st = lax.fori_loop(0, i - 1, far_body, scores(0))
    st = lax.fori_loop(jnp.maximum(i - 1, 0), i, off_body, st)
    softmax_pv(i, st + bdiag_ref[0])
    finish(acc_ref[...], l_ref[...])


def _diff_attention(proj, boff, bdiag, lam, gs, n_heads, score_bound):
    b, s, _ = proj.shape
    t = boff.shape[1]
    assert s % t == 0
    lam_row = jnp.full((1, t), lam, F32)
    gs_col = jnp.broadcast_to(gs.astype(F32)[:, None], (DV, t))
    operands = (proj, proj, proj, boff, bdiag, lam_row, gs_col)
    return lax.cond(score_bound <= ATTN_SAFE_LOG2,
                    functools.partial(_attn_call, t=t, n_heads=n_heads, bounded=True),
                    functools.partial(_attn_call, t=t, n_heads=n_heads, bounded=False),
                    *operands)


def _attn_call(*operands, t, n_heads, bounded):
    b, s, _ = operands[0].shape
    return pl.pallas_call(
        functools.partial(_attn_kernel, t=t, bounded=bounded),
        grid=(b, n_heads, s // t),
        in_specs=[
            pl.BlockSpec((1, t, DV), lambda bi, h, i: (bi, i, h)),
            pl.BlockSpec((1, s, DV), lambda bi, h, i: (bi, 0, n_heads + h)),
            pl.BlockSpec((1, s, DV), lambda bi, h, i: (bi, 0, 2 * n_heads + h)),
            pl.BlockSpec((1, t, 2 * t), lambda bi, h, i: (h, 0, 0)),
            pl.BlockSpec((1, t, 2 * t), lambda bi, h, i: (h, 0, 0)),
            pl.BlockSpec((1, t), lambda bi, h, i: (0, 0)),
            pl.BlockSpec((DV, t), lambda bi, h, i: (0, 0)),
        ],
        out_specs=pl.BlockSpec((1, t, DV), lambda bi, h, i: (bi, i, h)),
        out_shape=jax.ShapeDtypeStruct((b, s, n_heads * DV), BF16),
        scratch_shapes=[
            pltpu.VMEM((2 * t, DV), BF16),
            pltpu.VMEM((s // t, DV, t), BF16),
            pltpu.VMEM((1, 2 * t), F32),
            pltpu.VMEM((1, 2 * t), F32),
            pltpu.VMEM((DV, 2 * t), F32),
        ],
        compiler_params=_cparams(("parallel", "parallel", "arbitrary")),
        name="diff_attention_bounded" if bounded else "diff_attention",
    )(*operands)


def _conv_kernel(a_ref, gate_ref, w_ref, b_ref, lg_ref, lb_ref, o_ref, uext_ref, y_ref, win_ref, *, ts):
    si = pl.program_id(1)

    @pl.when(si == 0)
    def _():
        uext_ref[0:CONV_HALO, :] = jnp.zeros((CONV_HALO, uext_ref.shape[1]), F32)

    @pl.when(si > 0)
    def _():
        uext_ref[0:CONV_HALO, :] = uext_ref[ts:ts + CONV_HALO, :]

    a = a_ref[0].astype(F32)
    gate = gate_ref[0].astype(F32)
    uext_ref[CONV_HALO:CONV_HALO + ts, :] = a * _sigmoid(gate)

    base = CONV_HALO - (CONV_WIDTH - 1)
    rows = min(ts, CONV_ROWS)
    for c in range(uext_ref.shape[1] // LANES):
        cs = slice(c * LANES, (c + 1) * LANES)
        for r0 in range(0, ts, rows):
            acc = jnp.broadcast_to(b_ref[:, cs], (rows, LANES))
            for sh in range(SUBLANES):
                taps = [j for j in range(CONV_WIDTH) if (base + j) % SUBLANES == sh]
                span = max((base + j) // SUBLANES for j in taps) * SUBLANES + rows
                if sh:
                    win_ref[sh, 0:span, :] = uext_ref[r0 + sh:r0 + sh + span, cs]
                    win = win_ref[sh, 0:span, :]
                else:
                    win = uext_ref[r0:r0 + span, cs]
                for j in taps:
                    a0 = (base + j) // SUBLANES * SUBLANES
                    acc = acc + w_ref[j:j + 1, cs] * win[a0:a0 + rows]
            y_ref[r0:r0 + rows, cs] = acc

    y = y_ref[...]
    mu = jnp.mean(y, axis=-1, keepdims=True)
    yc = y - mu
    var = jnp.mean(yc * yc, axis=-1, keepdims=True)
    z = yc * lax.rsqrt(var + EPS) * lg_ref[...] + lb_ref[...]
    o_ref[0] = (z * _sigmoid(z)).astype(o_ref.dtype)


def _conformer_conv(proj, col0, conv_w, conv_b, ln_g, ln_b):
    b, s, _ = proj.shape
    c = conv_w.shape[1]
    ts = min(CONV_TS, s)
    assert s % ts == 0 and col0 % c == 0 and ts >= CONV_HALO
    return pl.pallas_call(
        functools.partial(_conv_kernel, ts=ts),
        grid=(b, s // ts),
        in_specs=[
            pl.BlockSpec((1, ts, c), lambda bi, si: (bi, si, col0 // c)),
            pl.BlockSpec((1, ts, c), lambda bi, si: (bi, si, col0 // c + 1)),
            pl.BlockSpec((CONV_WIDTH, c), lambda bi, si: (0, 0)),
            pl.BlockSpec((1, c), lambda bi, si: (0, 0)),
            pl.BlockSpec((1, c), lambda bi, si: (0, 0)),
            pl.BlockSpec((1, c), lambda bi, si: (0, 0)),
        ],
        out_specs=pl.BlockSpec((1, ts, c), lambda bi, si: (bi, si, 0)),
        out_shape=jax.ShapeDtypeStruct((b, s, c), BF16),
        scratch_shapes=[pltpu.VMEM((CONV_HALO + ts, c), F32), pltpu.VMEM((ts, c), F32),
                        pltpu.VMEM((SUBLANES, CONV_HALO + min(ts, CONV_ROWS), LANES), F32)],
        compiler_params=_cparams(("parallel", "arbitrary")),
        name="conformer_conv",
    )(proj, proj, conv_w, conv_b.reshape(1, c), ln_g.reshape(1, c), ln_b.reshape(1, c))


def _outproj_kernel(x_ref, a_ref, c_ref, w_ref, o_ref):
    ka = a_ref.shape[1]
    y = jnp.dot(a_ref[...], w_ref[0:ka, :], preferred_element_type=F32)
    y = y + jnp.dot(c_ref[...], w_ref[ka:, :], preferred_element_type=F32)
    o_ref[...] = x_ref[...] + y


def _outproj_residual(x, attn, conv, w, in_place):
    m, d = x.shape
    ka, kc = attn.shape[1], conv.shape[1]
    tm = min(PROJ_TM, m)
    tn = min(2 * PROJ_TN, d)
    assert m % tm == 0 and d % tn == 0
    return pl.pallas_call(
        _outproj_kernel,
        grid=(m // tm, d // tn),
        in_specs=[
            pl.BlockSpec((tm, tn), lambda i, j: (i, j)),
            pl.BlockSpec((tm, ka), lambda i, j: (i, 0)),
            pl.BlockSpec((tm, kc), lambda i, j: (i, 0)),
            pl.BlockSpec((ka + kc, tn), lambda i, j: (0, j)),
        ],
        out_specs=pl.BlockSpec((tm, tn), lambda i, j: (i, j)),
        out_shape=jax.ShapeDtypeStruct((m, d), F32),
        input_output_aliases={0: 0} if in_place else {},
        compiler_params=_cparams(("parallel", "parallel")),
        name="outproj_residual",
    )(x, attn, conv, w)


def _cross_kernel(x_ref, gc_ref, wq_ref, kv_ref, gq_ref, gk_ref, wo_ref, gf_ref,
                  wrh_ref, wrl_ref, br_ref, xo_ref, hp_ref, rt_ref):
    x = x_ref[0]
    d = x.shape[1]
    tq = x.shape[0]
    h = (x * lax.rsqrt(jnp.mean(x * x, axis=-1, keepdims=True) + EPS) * gc_ref[...]).astype(BF16)
    q = jnp.dot(h, wq_ref[...], preferred_element_type=F32)
    dc = N_HEADS_C * DH_C
    outs = []
    for hd in range(N_HEADS_C):
        cs = slice(hd * DH_C, (hd + 1) * DH_C)
        qh = q[:, cs]
        qh = qh * lax.rsqrt(jnp.mean(qh * qh, axis=-1, keepdims=True) + EPS) * gq_ref[...]
        kh = kv_ref[0, :, cs].astype(F32)
        kh = kh * lax.rsqrt(jnp.mean(kh * kh, axis=-1, keepdims=True) + EPS) * gk_ref[...]
        vh = kv_ref[0, :, dc + hd * DH_C:dc + (hd + 1) * DH_C]
        s = lax.dot_general(qh.astype(BF16), kh.astype(BF16), (((1,), (1,)), ((), ())),
                            preferred_element_type=F32)
        p = jnp.exp(s - jnp.max(s, axis=-1, keepdims=True))
        p = p / jnp.sum(p, axis=-1, keepdims=True)
        outs.append(jnp.dot(p.astype(BF16), vh, preferred_element_type=F32))
    o = jnp.concatenate(outs, axis=1).astype(BF16)
    x2 = x + jnp.dot(o, wo_ref[...], preferred_element_type=F32)
    xo_ref[0] = x2

    h2 = x2 * lax.rsqrt(jnp.mean(x2 * x2, axis=-1, keepdims=True) + EPS) * gf_ref[...]
    h2_hi = h2.astype(BF16)
    h2_hi32 = h2_hi.astype(F32)
    h2_lo = (h2 - h2_hi32).astype(BF16)

    _store_token_major(hp_ref, _pack_bf16_pair(h2_hi32))

    nt = (((1,), (1,)), ((), ()))
    lt = (lax.dot_general(wrh_ref[...], h2_hi, nt, preferred_element_type=F32)
          + lax.dot_general(wrh_ref[...], h2_lo, nt, preferred_element_type=F32)
          + lax.dot_general(wrl_ref[...], h2_hi, nt, preferred_element_type=F32)) + br_ref[...]
    g, e = N_GROUPS, EXPERTS_PER_GROUP
    gl = lt[0:g]
    rowg = lax.broadcasted_iota(jnp.int32, gl.shape, 0)
    gmax = jnp.max(gl, axis=0, keepdims=True)
    gsel = jnp.min(jnp.where(gl == gmax, rowg, g), axis=0, keepdims=True)
    gw = 1.0 / jnp.sum(jnp.exp(gl - gmax), axis=0, keepdims=True)
    el = jnp.zeros((e, tq), F32)
    for gi in range(g):
        el = jnp.where(gsel == gi, lt[g + gi * e:g + (gi + 1) * e], el)
    rowe = lax.broadcasted_iota(jnp.int32, el.shape, 0)
    v1 = jnp.max(el, axis=0, keepdims=True)
    i1 = jnp.min(jnp.where(el == v1, rowe, e), axis=0, keepdims=True)
    el2 = jnp.where(rowe == i1, -jnp.inf, el)
    v2 = jnp.max(el2, axis=0, keepdims=True)
    i2 = jnp.min(jnp.where(el2 == v2, rowe, e), axis=0, keepdims=True)
    e2 = jnp.exp(v2 - v1)
    den = 1.0 / (1.0 + e2)
    id1 = (gsel * e + i1).astype(F32)
    id2 = (gsel * e + i2).astype(F32)
    row8 = lax.broadcasted_iota(jnp.int32, (8, tq), 0)
    rt = jnp.where(row8 == 0, id1, jnp.where(row8 == 1, id2,
         jnp.where(row8 == 2, gw * den, jnp.where(row8 == 3, gw * e2 * den, 0.0))))
    rt_ref[0] = rt


def _cross_block(x, kv, g_cross, wq, g_qc, g_kc, wo, g_ffn, wr_hi, wr_lo, b_r):
    b, s, d = x.shape
    mlen = kv.shape[1]
    tq = min(CROSS_TQ, s)
    nq = s // tq
    dc = N_HEADS_C * DH_C
    nr = wr_hi.shape[0]
    assert s % tq == 0
    const = lambda bi, qi: (0, 0)
    return pl.pallas_call(
        _cross_kernel,
        grid=(b, nq),
        in_specs=[
            pl.BlockSpec((1, tq, d), lambda bi, qi: (bi, qi, 0)),
            pl.BlockSpec((1, d), const),
            pl.BlockSpec((d, dc), const),
            pl.BlockSpec((1, mlen, 2 * dc), lambda bi, qi: (bi, 0, 0)),
            pl.BlockSpec((1, DH_C), const),
            pl.BlockSpec((1, DH_C), const),
            pl.BlockSpec((dc, d), const),
            pl.BlockSpec((1, d), const),
            pl.BlockSpec((nr, d), const),
            pl.BlockSpec((nr, d), const),
            pl.BlockSpec((nr, 1), const),
        ],
        out_specs=[
            pl.BlockSpec((1, tq, d), lambda bi, qi: (bi, qi, 0)),
            pl.BlockSpec((tq * SUBLANES, LANES), lambda bi, qi: (bi * nq + qi, 0)),
            pl.BlockSpec((1, 8, tq), lambda bi, qi: (bi * nq + qi, 0, 0)),
        ],
        out_shape=[
            jax.ShapeDtypeStruct((b, s, d), F32),
            jax.ShapeDtypeStruct((b * s * SUBLANES, LANES), jnp.uint32),
            jax.ShapeDtypeStruct((b * nq, 8, tq), F32),
        ],
        input_output_aliases={0: 0},
        compiler_params=_cparams(("parallel", "parallel")),
        name="cross_block",
    )(x, g_cross.reshape(1, d), wq, kv, (g_qc * DH_C ** -0.5).reshape(1, DH_C), g_kc.reshape(1, DH_C),
      wo, g_ffn.reshape(1, d), wr_hi, wr_lo, b_r)


def _expert_kernel(idx_ref, be_ref, nr_ref, h_hbm, w1_ref, w3_ref, w2_ref, g_ref, z_hbm,
                   xbuf0, xbuf1, xbuf2, ybuf0, ybuf1, ybuf2, w1b, w3b, w2b, gsem, ssem, *, blk, n_steps):
    i = pl.program_id(0)
    n_cur = nr_ref[i]
    i_prev = jnp.maximum(i - 1, 0)
    n_prev = jnp.where(i >= 1, nr_ref[i_prev], 0)
    n_prev2 = jnp.where(i >= 2, nr_ref[jnp.maximum(i - 2, 0)], 0)
    slot = i % MOE_BUFS
    tok_mask = (1 << TOK_BITS) - 1
    xbuf = (xbuf0, xbuf1, xbuf2)
    ybuf = (ybuf0, ybuf1, ybuf2)
    rs = SUBLANES

    def tile_rows(row):
        return pl.ds(pl.multiple_of(row * rs, rs), rs)

    def gather_copy(step, r, sl):
        row = idx_ref[step * blk + r] & tok_mask
        return pltpu.make_async_copy(h_hbm.at[tile_rows(row)], xbuf[sl].at[tile_rows(r)], gsem.at[sl])

    def scatter_row(step, r):
        return lax.shift_right_logical(idx_ref[step * blk + r], TOK_BITS)

    def scatter_copy(row, r, sl):
        return pltpu.make_async_copy(ybuf[sl].at[tile_rows(r)], z_hbm.at[tile_rows(row)], ssem.at[sl])

    def issue_gather(step, sl):
        def body(r, carry):
            gather_copy(step, r, sl).start()
            return carry
        lax.fori_loop(0, blk, body, 0, unroll=8)

    def wait_gather(sl):
        pltpu.make_async_copy(h_hbm.at[pl.ds(0, blk * rs)], xbuf[sl], gsem.at[sl]).wait()

    def issue_scatter(step, sl, n):
        def body(r, carry):
            row = scatter_row(step, r)

            @pl.when(r < n)
            def _():
                scatter_copy(row, r, sl).start()
            return carry
        lax.fori_loop(0, blk, body, 0, unroll=8)

    def wait_scatter(sl, n):
        k = blk
        while k >= 1:
            @pl.when((n & k) != 0)
            def _(k=k):
                pltpu.make_async_copy(ybuf[sl].at[pl.ds(0, k * rs)], z_hbm.at[pl.ds(0, k * rs)],
                                      ssem.at[sl]).wait()
            k //= 2

    @pl.when(i == 0)
    def _():
        issue_gather(0, 0)
        issue_gather(1, 1)

    @pl.when((n_cur > 0) & ((i == 0) | (be_ref[i] != be_ref[jnp.maximum(i - 1, 0)])))
    def _():
        w1b[...] = w1_ref[0, 0].astype(BF16)
        w3b[...] = w3_ref[0, 0].astype(BF16)
        w2b[...] = w2_ref[0, 0].astype(BF16)

    def block_body(sl):
        nxt, prv = (sl + 1) % MOE_BUFS, (sl + 2) % MOE_BUFS
        wait_gather(sl)
        xl, xr = _unpack_bf16_pair(_load_token_major(xbuf[sl], blk))
        xl, xr = xl.astype(BF16), xr.astype(BF16)
        dh = xl.shape[1]
        for r in range(blk):
            gather_copy(i + 2, r, prv).start(priority=ROW_DMA_PRIORITY)
        for r in range(blk):
            row = scatter_row(i_prev, r)

            @pl.when(r < n_prev)
            def _(r=r, row=row):
                scatter_copy(row, r, prv).start(priority=ROW_DMA_PRIORITY)
        a1 = (jnp.dot(xl, w1b[0:dh, :], preferred_element_type=F32)
              + jnp.dot(xr, w1b[dh:, :], preferred_element_type=F32))
        a3 = (jnp.dot(xl, w3b[0:dh, :], preferred_element_type=F32)
              + jnp.dot(xr, w3b[dh:, :], preferred_element_type=F32))
        hid = (a1 * _sigmoid(a1) * a3 * g_ref[...]).astype(BF16)
        y = jnp.dot(hid, w2b[...], preferred_element_type=F32)
        wait_scatter(nxt, n_prev2)
        _store_token_major(ybuf[sl], _pack_bf16_pair(y))

        @pl.when(i == n_steps - 1)
        def _():
            wait_gather(nxt)
            wait_gather(prv)
            wait_scatter(prv, n_prev)
            issue_scatter(i, sl, n_cur)
            wait_scatter(sl, n_cur)

    for sl in range(MOE_BUFS):
        @pl.when((n_cur > 0) & (slot == sl))
        def _(sl=sl):
            block_body(sl)

    for sl in range(MOE_BUFS):
        @pl.when((n_cur == 0) & (n_prev > 0) & (slot == sl))
        def _(sl=sl):
            nxt, prv = (sl + 1) % MOE_BUFS, (sl + 2) % MOE_BUFS
            wait_gather(sl)
            wait_gather(nxt)
            wait_scatter(nxt, n_prev2)
            issue_scatter(i_prev, prv, n_prev)
            wait_scatter(prv, n_prev)


def _expert_mlp(hpack, idx, blk_expert, nreal, gate_buf, w1, w3, w2, layer, z_rows):
    d = w1.shape[2]
    assert d == 2 * SUBLANES * LANES and hpack.shape[1] == LANES
    n_steps = idx.shape[0] // MOE_BLK - (MOE_BUFS - 1)
    assert n_steps >= MOE_BUFS
    de = w1.shape[3]
    assert MOE_BLK & (MOE_BLK - 1) == 0
    return pl.pallas_call(
        functools.partial(_expert_kernel, blk=MOE_BLK, n_steps=n_steps),
        grid_spec=pltpu.PrefetchScalarGridSpec(
            num_scalar_prefetch=3,
            grid=(n_steps,),
            in_specs=[
                pl.BlockSpec(memory_space=pl.ANY),
                pl.BlockSpec((1, 1, d, de), lambda i, ix, be, nr: (layer, be[i], 0, 0)),
                pl.BlockSpec((1, 1, d, de), lambda i, ix, be, nr: (layer, be[i], 0, 0)),
                pl.BlockSpec((1, 1, de, d), lambda i, ix, be, nr: (layer, be[i], 0, 0)),
                pl.BlockSpec((MOE_BLK, 1), lambda i, ix, be, nr: (i, 0)),
            ],
            out_specs=pl.BlockSpec(memory_space=pl.ANY),
            scratch_shapes=(
                [pltpu.VMEM((MOE_BLK * SUBLANES, LANES), jnp.uint32)] * (2 * MOE_BUFS) + [
                    pltpu.VMEM((d, de), BF16),
                    pltpu.VMEM((d, de), BF16),
                    pltpu.VMEM((de, d), BF16),
                    pltpu.SemaphoreType.DMA((MOE_BUFS,)),
                    pltpu.SemaphoreType.DMA((MOE_BUFS,)),
                ]),
        ),
        out_shape=jax.ShapeDtypeStruct((z_rows * SUBLANES, LANES), jnp.uint32),
        compiler_params=_cparams(("arbitrary",)),
        name="expert_mlp",
    )(idx, blk_expert, nreal, hpack, w1, w3, w2, gate_buf)


def _combine_kernel(x_ref, *refs):
    z_refs, o_ref = refs[:-1], refs[-1]
    tm, d = x_ref.shape
    acc_hi = x_ref[:, 0:d // 2]
    acc_lo = x_ref[:, d // 2:]
    for z_ref in z_refs:
        hi, lo = _unpack_bf16_pair(_load_token_major(z_ref, tm))
        acc_hi = acc_hi + hi
        acc_lo = acc_lo + lo
    o_ref[:, 0:d // 2] = acc_hi
    o_ref[:, d // 2:] = acc_lo


def _combine(x, z):
    t, d = x.shape
    tm = min(COMB_TM, t)
    assert t % tm == 0
    nt = t // tm
    return pl.pallas_call(
        _combine_kernel,
        grid=(nt,),
        in_specs=[pl.BlockSpec((tm, d), lambda i: (i, 0))] + [
            pl.BlockSpec((tm * SUBLANES, LANES), lambda i, k=k: (k * nt + i, 0)) for k in range(TOP_K)],
        out_specs=pl.BlockSpec((tm, d), lambda i: (i, 0)),
        out_shape=jax.ShapeDtypeStruct((t, d), F32),
        input_output_aliases={0: 0},
        compiler_params=_cparams(("parallel",)),
        name="moe_combine",
    )(x, *([z] * TOP_K))


def _t5_causal_bucket(dist):
    max_exact = N_BUCKETS // 2
    d_f = jnp.maximum(dist, 1).astype(F32)
    large = max_exact + (jnp.log(d_f / max_exact) / math.log(MAX_DISTANCE / max_exact)
                         * (N_BUCKETS - max_exact)).astype(jnp.int32)
    large = jnp.minimum(large, N_BUCKETS - 1)
    return jnp.where(dist < max_exact, dist, large)


def _bias_tiles(rel_bias_table, s, t):
    assert t >= MAX_DISTANCE
    rbd = rel_bias_table[_t5_causal_bucket(jnp.arange(s))].T.astype(F32)
    far = rel_bias_table[N_BUCKETS - 1].astype(F32)
    rbd = rbd - far[:, None]
    h = rbd.shape[0]

    def toeplitz(w):
        a = jnp.broadcast_to(w[:, None, :], (h, t, 2 * t)).reshape(h, 2 * t * t)
        return a[:, :t * (2 * t - 1)].reshape(h, t, 2 * t - 1)[:, :, :t]

    k = np.arange(2 * t)
    d_diag = np.where(k == 0, 0, np.minimum(2 * t - k, s - 1))
    w_diag = jnp.where((k >= 1) & (k <= t), NEG_INF, rbd[:, d_diag])
    d_off = np.minimum(np.where(k < t, t - k, 3 * t - k), s - 1)
    w_off = rbd[:, d_off]
    off = toeplitz(w_off).transpose(0, 2, 1) * LOG2E
    diag = jnp.maximum(toeplitz(w_diag).transpose(0, 2, 1) * LOG2E, NEG_INF)
    return jnp.concatenate([off, off], axis=2), jnp.concatenate([diag, diag], axis=2)


def _moe_plan(rt, t, n_experts, blk):
    nblk_rt, _, tq = rt.shape
    ids = rt[:, 0:TOP_K, :].astype(jnp.int32).transpose(0, 2, 1).reshape(t * TOP_K)
    gates = rt[:, TOP_K:2 * TOP_K, :].transpose(0, 2, 1).reshape(t * TOP_K)
    a = t * TOP_K
    order = jnp.argsort(ids).astype(jnp.int32)
    counts = jnp.sum((ids[None, :] == jnp.arange(n_experts)[:, None]).astype(jnp.int32), axis=1)
    start = jnp.cumsum(counts) - counts
    pcounts = (counts + blk - 1) // blk * blk
    pend = jnp.cumsum(pcounts)
    pstart = pend - pcounts
    n_blocks = -(-(a + n_experts * (blk - 1)) // blk)
    p = n_blocks * blk
    blk_row0 = jnp.arange(n_blocks, dtype=jnp.int32) * blk
    blk_expert = jnp.minimum(jnp.sum((pend[None, :] <= blk_row0[:, None]).astype(jnp.int32), axis=1),
                             n_experts - 1)
    nreal = jnp.clip((pstart + counts)[blk_expert] - blk_row0, 0, blk).astype(jnp.int32)
    within = (blk_row0 - pstart[blk_expert])[:, None] + jnp.arange(blk, dtype=jnp.int32)[None, :]
    valid = jnp.arange(blk, dtype=jnp.int32)[None, :] < nreal[:, None]
    src = jnp.where(valid, start[blk_expert][:, None] + within, 0).reshape(p)
    valid = valid.reshape(p)
    asg = order[src].astype(jnp.uint32)
    zrow = (asg % TOP_K) * t + asg // TOP_K
    idx = jnp.where(valid, (zrow << TOK_BITS) | (asg // TOP_K), 0).astype(jnp.uint32)
    idx = lax.bitcast_convert_type(idx, jnp.int32)
    idx = jnp.concatenate([idx, jnp.zeros(((MOE_BUFS - 1) * blk,), jnp.int32)])
    gate_buf = jnp.where(valid, gates[order[src]], 0.0)
    nreal = jnp.concatenate([nreal, jnp.zeros((1,), jnp.int32)])
    return idx, blk_expert.astype(jnp.int32), nreal, gate_buf.reshape(p, 1)


def kernel(x, mem, rel_bias_table, g_mix, w_in, g_q, g_k, diff_lambda, g_subln, conv_w, conv_b,
           conv_ln_g, conv_ln_b, w_out, g_cross, g_mem, wq_c, wkv_c, g_qc, g_kc, wo_c, g_ffn,
           w_group, b_group, w_router, b_router, w1, w3, w2):
    b, s, d = x.shape
    mlen = mem.shape[1]
    depth = w_in.shape[0]
    t = b * s
    d_conv = conv_w.shape[2]
    n_heads = (w_out.shape[1] - d_conv) // DV
    d_qk = n_heads * 2 * DQ
    n_experts = w_router.shape[2]
    assert t <= (1 << TOK_BITS) and TOP_K * t <= (1 << (32 - TOK_BITS))

    boff, bdiag = _bias_tiles(rel_bias_table, s, min(ATTN_T, s))
    bias_abs = jnp.maximum(jnp.max(jnp.abs(boff)),
                           jnp.max(jnp.where(bdiag > 0.5 * NEG_INF, jnp.abs(bdiag), 0.0)))
    n_router_rows = -(-(N_GROUPS + n_experts) // LANES) * LANES

    for l in range(depth):
        lam_init = 0.8 - 0.6 * math.exp(-0.3 * l)
        dl = diff_lambda[l].astype(F32)
        lam = jnp.exp(jnp.sum(dl[0] * dl[1])) - jnp.exp(jnp.sum(dl[2] * dl[3])) + lam_init
        gs = g_subln[l] * (1.0 - lam_init)
        qk_gain = jnp.concatenate([
            jnp.tile(g_q[l] * (DQ ** -0.5 * LOG2E), d_qk // DQ), jnp.tile(g_k[l], d_qk // DQ),
            jnp.ones((w_in.shape[2] - 2 * d_qk,), F32)]).reshape(1, -1)

        proj = _norm_matmul(x.reshape(t, d), g_mix[l], w_in[l].astype(BF16), qk_gain, 2 * d_qk)
        proj = proj.reshape(b, s, -1)
        score_bound = (jnp.max(jnp.abs(g_q[l])) * jnp.max(jnp.abs(g_k[l])) * (math.sqrt(DQ) * LOG2E * 1.02)
                       + bias_abs)
        attn = _diff_attention(proj, boff, bdiag, lam, gs, n_heads, score_bound)
        conv = _conformer_conv(proj, 2 * d_qk + n_heads * DV, conv_w[l], conv_b[l],
                               conv_ln_g[l], conv_ln_b[l])
        x = _outproj_residual(x.reshape(t, d), attn.reshape(t, -1), conv.reshape(t, -1),
                              w_out[l].astype(BF16), in_place=l > 0).reshape(b, s, d)

        kv = _norm_matmul(mem.reshape(b * mlen, d), g_mem[l], wkv_c[l].astype(BF16))
        wr = jnp.zeros((n_router_rows, d), F32)
        wr = wr.at[0:N_GROUPS].set(w_group[l].T).at[N_GROUPS:N_GROUPS + n_experts].set(w_router[l].T)
        wr_hi = wr.astype(BF16)
        wr_lo = (wr - wr_hi.astype(F32)).astype(BF16)
        b_r = jnp.zeros((n_router_rows, 1), F32)
        b_r = b_r.at[0:N_GROUPS, 0].set(b_group[l]).at[N_GROUPS:N_GROUPS + n_experts, 0].set(b_router[l])
        x, hpack, rt = _cross_block(x, kv.reshape(b, mlen, -1), g_cross[l], wq_c[l].astype(BF16),
                                    g_qc[l], g_kc[l], wo_c[l].astype(BF16), g_ffn[l], wr_hi, wr_lo, b_r)

        idx, blk_expert, nreal, gate_buf = _moe_plan(rt, t, n_experts, MOE_BLK)
        z = _expert_mlp(hpack, idx, blk_expert, nreal, gate_buf, w1, w3, w2, l, TOP_K * t)
        x = _combine(x.reshape(t, d), z).reshape(b, s, d)
    return x
```

```python
import functools
import math

import jax
import jax.numpy as jnp
import numpy as np
from jax import lax
from jax.experimental import pallas as pl
from jax.experimental.pallas import tpu as pltpu

DQ = 64
DV = 2 * DQ
CONV_WIDTH = 31
N_BUCKETS = 32
MAX_DISTANCE = 128
N_HEADS_C = 4
DH_C = 128
N_GROUPS = 8
EXPERTS_PER_GROUP = 8
TOP_K = 2
EPS = 1e-6
NEG_INF = -1e30
LOG2E = 1.4426950408889634
ATTN_SAFE_LOG2 = 96.0

LANES = 128
SUBLANES = 8
MXU_DIM = 256
VMEM_LIMIT_BYTES = 56 * 1024 * 1024

F32 = jnp.float32
BF16 = jnp.bfloat16

PROJ_TM = 1024
PROJ_TN = 512
ATTN_T = 256
CONV_TS = 256
CONV_HALO = 32
CONV_ROWS = 128
CROSS_TQ = 512
MOE_BLK = 256
COMB_TM = 256
TOK_BITS = 15
ROW_DMA_PRIORITY = 1
MOE_BUFS = 3


def _cparams(sem):
    return pltpu.CompilerParams(dimension_semantics=sem, vmem_limit_bytes=VMEM_LIMIT_BYTES)


def _sigmoid(x):
    return 1.0 / (1.0 + jnp.exp(-x))


def _store_token_major(ref, val):
    n = val.shape[0]
    for c in range(SUBLANES):
        ref[pl.ds(c, n, stride=SUBLANES), :] = val[:, c * LANES:(c + 1) * LANES]


def _load_token_major(ref, n):
    return jnp.concatenate([ref[pl.ds(c, n, stride=SUBLANES), :] for c in range(SUBLANES)], axis=1)


def _pack_bf16_pair(x):
    w = x.shape[1] // 2
    bits = lax.bitcast_convert_type(x.astype(BF16).astype(F32), jnp.uint32)
    return bits[:, 0:w] | (bits[:, w:] >> 16)


def _unpack_bf16_pair(word):
    hi = lax.bitcast_convert_type(word & jnp.uint32(0xFFFF0000), F32)
    lo = lax.bitcast_convert_type(word << 16, F32)
    return hi, lo


def _norm_matmul_kernel(x_ref, g_ref, w_ref, qkg_ref, seg_ref, o_ref, xn_ref, *, n_qk_tiles):
    j = pl.program_id(1)

    @pl.when(j == 0)
    def _():
        x = x_ref[...]
        inv = lax.rsqrt(jnp.mean(x * x, axis=-1, keepdims=True) + EPS)
        xn_ref[...] = (x * inv * g_ref[...]).astype(BF16)

    acc = jnp.dot(xn_ref[...], w_ref[...], preferred_element_type=F32)

    if n_qk_tiles == 0:
        o_ref[...] = acc.astype(o_ref.dtype)
        return

    @pl.when(j < n_qk_tiles)
    def _():
        sq = (acc * acc).astype(BF16)
        seg = seg_ref[...]
        parts = [jnp.dot(sq[:, c * MXU_DIM:(c + 1) * MXU_DIM], seg, preferred_element_type=F32)
                 for c in range(acc.shape[1] // MXU_DIM)]
        ss = jnp.concatenate(parts, axis=1)
        o_ref[...] = (acc * lax.rsqrt(ss * (1.0 / DQ) + EPS) * qkg_ref[...]).astype(o_ref.dtype)

    @pl.when(j >= n_qk_tiles)
    def _():
        o_ref[...] = acc.astype(o_ref.dtype)


def _norm_matmul(x, g, w, qk_gain=None, n_qk_cols=0):
    m, d = x.shape
    n = w.shape[1]
    tm = min(PROJ_TM, m)
    tn = min(PROJ_TN, n)
    assert m % tm == 0 and n % tn == 0 and n_qk_cols % tn == 0 and tn % MXU_DIM == 0
    if qk_gain is None:
        qk_gain = jnp.ones((1, n), F32)
    r = jnp.arange(MXU_DIM) // DQ
    seg = (r[:, None] == r[None, :]).astype(BF16)
    return pl.pallas_call(
        functools.partial(_norm_matmul_kernel, n_qk_tiles=n_qk_cols // tn),
        grid=(m // tm, n // tn),
        in_specs=[
            pl.BlockSpec((tm, d), lambda i, j: (i, 0)),
            pl.BlockSpec((1, d), lambda i, j: (0, 0)),
            pl.BlockSpec((d, tn), lambda i, j: (0, j)),
            pl.BlockSpec((1, tn), lambda i, j: (0, j)),
            pl.BlockSpec((MXU_DIM, MXU_DIM), lambda i, j: (0, 0)),
        ],
        out_specs=pl.BlockSpec((tm, tn), lambda i, j: (i, j)),
        out_shape=jax.ShapeDtypeStruct((m, n), BF16),
        scratch_shapes=[pltpu.VMEM((tm, d), BF16)],
        compiler_params=_cparams(("parallel", "arbitrary")),
        name="norm_matmul",
    )(x, g.reshape(1, d), w, qk_gain, seg)


def _attn_kernel(q_ref, k_ref, v_ref, boff_ref, bdiag_ref, lam_ref, gs_ref, o_ref,
                 qs_ref, vt_ref, m_ref, l_ref, acc_ref, *, t, bounded):
    i = pl.program_id(2)
    s_len = k_ref.shape[1]

    @pl.when(i == 0)
    def _():
        for c in range(s_len // t):
            vt_ref[c] = v_ref[0, c * t:(c + 1) * t, :].astype(F32).T.astype(BF16)

    q = q_ref[0].astype(F32)
    lane = lax.broadcasted_iota(jnp.int32, q.shape, 1)
    qs_ref[0:t, :] = jnp.where(lane < DQ, q, 0.0).astype(BF16)
    qs_ref[t:2 * t, :] = jnp.where(lane >= DQ, q, 0.0).astype(BF16)
    def scores(j):
        off = j * t if isinstance(j, int) else pl.multiple_of(j * t, t)
        k = k_ref[0, pl.ds(off, t), :]
        return lax.dot_general(k, qs_ref[...], (((1,), (1,)), ((), ())), preferred_element_type=F32)

    def finish(acc, l):
        ot = acc / l
        ot = ot[:, 0:t] - lam_ref[...] * ot[:, t:2 * t]
        inv = lax.rsqrt(jnp.mean(ot * ot, axis=0, keepdims=True) + EPS)
        o_ref[0] = (ot * inv * gs_ref[...]).T.astype(o_ref.dtype)

    if bounded:
        def sweep(ii):
            acc = l8 = None
            for j in range(ii + 1):
                st = scores(j)
                if j == ii:
                    st = st + bdiag_ref[0]
                elif j == ii - 1:
                    st = st + boff_ref[0]
                pt = jnp.exp2(st)
                lj = jnp.sum(pt.reshape(t // SUBLANES, SUBLANES, 2 * t), axis=0)
                pv = jnp.dot(vt_ref[j], pt.astype(BF16), preferred_element_type=F32)
                acc = pv if acc is None else acc + pv
                l8 = lj if l8 is None else l8 + lj
            finish(acc, jnp.sum(l8, axis=0, keepdims=True))

        for ii in range(s_len // t):
            pl.when(i == ii)(functools.partial(sweep, ii))
        return

    m_ref[...] = jnp.full(m_ref.shape, NEG_INF, F32)
    l_ref[...] = jnp.zeros(l_ref.shape, F32)
    acc_ref[...] = jnp.zeros(acc_ref.shape, F32)

    def softmax_pv(j, st):
        m_prev = m_ref[...]
        m_new = jnp.maximum(m_prev, jnp.max(st, axis=0, keepdims=True))
        alpha = jnp.exp2(m_prev - m_new)
        pt = jnp.exp2(st - m_new)
        l_ref[...] = alpha * l_ref[...] + jnp.sum(pt, axis=0, keepdims=True)
        pv = jnp.dot(vt_ref[j], pt.astype(BF16), preferred_element_type=F32)
        acc_ref[...] = alpha * acc_ref[...] + pv
        m_ref[...] = m_new

    def far_body(j, st):
        st_next = scores(j + 1)
        softmax_pv(j, st)
        return st_next

    def off_body(j, st):
        st_next = scores(j + 1)
        softmax_pv(j, st + boff_ref[0])
        return st_next

    st = lax.fori_loop(0, i - 1, far_body, scores(0))
    st = lax.fori_loop(jnp.maximum(i - 1, 0), i, off_body, st)
    softmax_pv(i, st + bdiag_ref[0])
    finish(acc_ref[...], l_ref[...])


def _diff_attention(proj, boff, bdiag, lam, gs, n_heads, score_bound):
    b, s, _ = proj.shape
    t = boff.shape[1]
    assert s % t == 0
    lam_row = jnp.full((1, t), lam, F32)
    gs_col = jnp.broadcast_to(gs.astype(F32)[:, None], (DV, t))
    operands = (proj, proj, proj, boff, bdiag, lam_row, gs_col)
    return lax.cond(score_bound <= ATTN_SAFE_LOG2,
                    functools.partial(_attn_call, t=t, n_heads=n_heads, bounded=True),
                    functools.partial(_attn_call, t=t, n_heads=n_heads, bounded=False),
                    *operands)


def _attn_call(*operands, t, n_heads, bounded):
    b, s, _ = operands[0].shape
    return pl.pallas_call(
        functools.partial(_attn_kernel, t=t, bounded=bounded),
        grid=(b, n_heads, s // t),
        in_specs=[
            pl.BlockSpec((1, t, DV), lambda bi, h, i: (bi, i, h)),
            pl.BlockSpec((1, s, DV), lambda bi, h, i: (bi, 0, n_heads + h)),
            pl.BlockSpec((1, s, DV), lambda bi, h, i: (bi, 0, 2 * n_heads + h)),
            pl.BlockSpec((1, t, 2 * t), lambda bi, h, i: (h, 0, 0)),
            pl.BlockSpec((1, t, 2 * t), lambda bi, h, i: (h, 0, 0)),
            pl.BlockSpec((1, t), lambda bi, h, i: (0, 0)),
            pl.BlockSpec((DV, t), lambda bi, h, i: (0, 0)),
        ],
        out_specs=pl.BlockSpec((1, t, DV), lambda bi, h, i: (bi, i, h)),
        out_shape=jax.ShapeDtypeStruct((b, s, n_heads * DV), BF16),
        scratch_shapes=[
            pltpu.VMEM((2 * t, DV), BF16),
            pltpu.VMEM((s // t, DV, t), BF16),
            pltpu.VMEM((1, 2 * t), F32),
            pltpu.VMEM((1, 2 * t), F32),
            pltpu.VMEM((DV, 2 * t), F32),
        ],
        compiler_params=_cparams(("parallel", "parallel", "arbitrary")),
        name="diff_attention_bounded" if bounded else "diff_attention",
    )(*operands)


def _conv_kernel(a_ref, gate_ref, w_ref, b_ref, lg_ref, lb_ref, o_ref, uext_ref, y_ref, win_ref, *, ts):
    si = pl.program_id(1)

    @pl.when(si == 0)
    def _():
        uext_ref[0:CONV_HALO, :] = jnp.zeros((CONV_HALO, uext_ref.shape[1]), F32)

    @pl.when(si > 0)
    def _():
        uext_ref[0:CONV_HALO, :] = uext_ref[ts:ts + CONV_HALO, :]

    a = a_ref[0].astype(F32)
    gate = gate_ref[0].astype(F32)
    uext_ref[CONV_HALO:CONV_HALO + ts, :] = a * _sigmoid(gate)

    base = CONV_HALO - (CONV_WIDTH - 1)
    rows = min(ts, CONV_ROWS)
    for c in range(uext_ref.shape[1] // LANES):
        cs = slice(c * LANES, (c + 1) * LANES)
        for r0 in range(0, ts, rows):
            acc = jnp.broadcast_to(b_ref[:, cs], (rows, LANES))
            for sh in range(SUBLANES):
                taps = [j for j in range(CONV_WIDTH) if (base + j) % SUBLANES == sh]
                span = max((base + j) // SUBLANES for j in taps) * SUBLANES + rows
                if sh:
                    win_ref[sh, 0:span, :] = uext_ref[r0 + sh:r0 + sh + span, cs]
                    win = win_ref[sh, 0:span, :]
                else:
                    win = uext_ref[r0:r0 + span, cs]
                for j in taps:
                    a0 = (base + j) // SUBLANES * SUBLANES
                    acc = acc + w_ref[j:j + 1, cs] * win[a0:a0 + rows]
            y_ref[r0:r0 + rows, cs] = acc

    y = y_ref[...]
    mu = jnp.mean(y, axis=-1, keepdims=True)
    yc = y - mu
    var = jnp.mean(yc * yc, axis=-1, keepdims=True)
    z = yc * lax.rsqrt(var + EPS) * lg_ref[...] + lb_ref[...]
    o_ref[0] = (z * _sigmoid(z)).astype(o_ref.dtype)


def _conformer_conv(proj, col0, conv_w, conv_b, ln_g, ln_b):
    b, s, _ = proj.shape
    c = conv_w.shape[1]
    ts = min(CONV_TS, s)
    assert s % ts == 0 and col0 % c == 0 and ts >= CONV_HALO
    return pl.pallas_call(
        functools.partial(_conv_kernel, ts=ts),
        grid=(b, s // ts),
        in_specs=[
            pl.BlockSpec((1, ts, c), lambda bi, si: (bi, si, col0 // c)),
            pl.BlockSpec((1, ts, c), lambda bi, si: (bi, si, col0 // c + 1)),
            pl.BlockSpec((CONV_WIDTH, c), lambda bi, si: (0, 0)),
            pl.BlockSpec((1, c), lambda bi, si: (0, 0)),
            pl.BlockSpec((1, c), lambda bi, si: (0, 0)),
            pl.BlockSpec((1, c), lambda bi, si: (0, 0)),
        ],
        out_specs=pl.BlockSpec((1, ts, c), lambda bi, si: (bi, si, 0)),
        out_shape=jax.ShapeDtypeStruct((b, s, c), BF16),
        scratch_shapes=[pltpu.VMEM((CONV_HALO + ts, c), F32), pltpu.VMEM((ts, c), F32),
                        pltpu.VMEM((SUBLANES, CONV_HALO + min(ts, CONV_ROWS), LANES), F32)],
        compiler_params=_cparams(("parallel", "arbitrary")),
        name="conformer_conv",
    )(proj, proj, conv_w, conv_b.reshape(1, c), ln_g.reshape(1, c), ln_b.reshape(1, c))


def _outproj_kernel(x_ref, a_ref, c_ref, w_ref, o_ref):
    ka = a_ref.shape[1]
    y = jnp.dot(a_ref[...], w_ref[0:ka, :], preferred_element_type=F32)
    y = y + jnp.dot(c_ref[...], w_ref[ka:, :], preferred_element_type=F32)
    o_ref[...] = x_ref[...] + y


def _outproj_residual(x, attn, conv, w, in_place):
    m, d = x.shape
    ka, kc = attn.shape[1], conv.shape[1]
    tm = min(PROJ_TM, m)
    tn = min(2 * PROJ_TN, d)
    assert m % tm == 0 and d % tn == 0
    return pl.pallas_call(
        _outproj_kernel,
        grid=(m // tm, d // tn),
        in_specs=[
            pl.BlockSpec((tm, tn), lambda i, j: (i, j)),
            pl.BlockSpec((tm, ka), lambda i, j: (i, 0)),
            pl.BlockSpec((tm, kc), lambda i, j: (i, 0)),
            pl.BlockSpec((ka + kc, tn), lambda i, j: (0, j)),
        ],
        out_specs=pl.BlockSpec((tm, tn), lambda i, j: (i, j)),
        out_shape=jax.ShapeDtypeStruct((m, d), F32),
        input_output_aliases={0: 0} if in_place else {},
        compiler_params=_cparams(("parallel", "parallel")),
        name="outproj_residual",
    )(x, attn, conv, w)


def _cross_kernel(x_ref, gc_ref, wq_ref, kv_ref, gq_ref, gk_ref, wo_ref, gf_ref,
                  wrh_ref, wrl_ref, br_ref, xo_ref, hp_ref, rt_ref):
    x = x_ref[0]
    d = x.shape[1]
    tq = x.shape[0]
    h = (x * lax.rsqrt(jnp.mean(x * x, axis=-1, keepdims=True) + EPS) * gc_ref[...]).astype(BF16)
    q = jnp.dot(h, wq_ref[...], preferred_element_type=F32)
    dc = N_HEADS_C * DH_C
    outs = []
    for hd in range(N_HEADS_C):
        cs = slice(hd * DH_C, (hd + 1) * DH_C)
        qh = q[:, cs]
        qh = qh * lax.rsqrt(jnp.mean(qh * qh, axis=-1, keepdims=True) + EPS) * gq_ref[...]
        kh = kv_ref[0, :, cs].astype(F32)
        kh = kh * lax.rsqrt(jnp.mean(kh * kh, axis=-1, keepdims=True) + EPS) * gk_ref[...]
        vh = kv_ref[0, :, dc + hd * DH_C:dc + (hd + 1) * DH_C]
        s = lax.dot_general(qh.astype(BF16), kh.astype(BF16), (((1,), (1,)), ((), ())),
                            preferred_element_type=F32)
        p = jnp.exp(s - jnp.max(s, axis=-1, keepdims=True))
        p = p / jnp.sum(p, axis=-1, keepdims=True)
        outs.append(jnp.dot(p.astype(BF16), vh, preferred_element_type=F32))
    o = jnp.concatenate(outs, axis=1).astype(BF16)
    x2 = x + jnp.dot(o, wo_ref[...], preferred_element_type=F32)
    xo_ref[0] = x2

    h2 = x2 * lax.rsqrt(jnp.mean(x2 * x2, axis=-1, keepdims=True) + EPS) * gf_ref[...]
    h2_hi = h2.astype(BF16)
    h2_hi32 = h2_hi.astype(F32)
    h2_lo = (h2 - h2_hi32).astype(BF16)

    _store_token_major(hp_ref, _pack_bf16_pair(h2_hi32))

    nt = (((1,), (1,)), ((), ()))
    lt = (lax.dot_general(wrh_ref[...], h2_hi, nt, preferred_element_type=F32)
          + lax.dot_general(wrh_ref[...], h2_lo, nt, preferred_element_type=F32)
          + lax.dot_general(wrl_ref[...], h2_hi, nt, preferred_element_type=F32)) + br_ref[...]
    g, e = N_GROUPS, EXPERTS_PER_GROUP
    gl = lt[0:g]
    rowg = lax.broadcasted_iota(jnp.int32, gl.shape, 0)
    gmax = jnp.max(gl, axis=0, keepdims=True)
    gsel = jnp.min(jnp.where(gl == gmax, rowg, g), axis=0, keepdims=True)
    gw = 1.0 / jnp.sum(jnp.exp(gl - gmax), axis=0, keepdims=True)
    el = jnp.zeros((e, tq), F32)
    for gi in range(g):
        el = jnp.where(gsel == gi, lt[g + gi * e:g + (gi + 1) * e], el)
    rowe = lax.broadcasted_iota(jnp.int32, el.shape, 0)
    v1 = jnp.max(el, axis=0, keepdims=True)
    i1 = jnp.min(jnp.where(el == v1, rowe, e), axis=0, keepdims=True)
    el2 = jnp.where(rowe == i1, -jnp.inf, el)
    v2 = jnp.max(el2, axis=0, keepdims=True)
    i2 = jnp.min(jnp.where(el2 == v2, rowe, e), axis=0, keepdims=True)
    e2 = jnp.exp(v2 - v1)
    den = 1.0 / (1.0 + e2)
    id1 = (gsel * e + i1).astype(F32)
    id2 = (gsel * e + i2).astype(F32)
    row8 = lax.broadcasted_iota(jnp.int32, (8, tq), 0)
    rt = jnp.where(row8 == 0, id1, jnp.where(row8 == 1, id2,
         jnp.where(row8 == 2, gw * den, jnp.where(row8 == 3, gw * e2 * den, 0.0))))
    rt_ref[0] = rt


def _cross_block(x, kv, g_cross, wq, g_qc, g_kc, wo, g_ffn, wr_hi, wr_lo, b_r):
    b, s, d = x.shape
    mlen = kv.shape[1]
    tq = min(CROSS_TQ, s)
    nq = s // tq
    dc = N_HEADS_C * DH_C
    nr = wr_hi.shape[0]
    assert s % tq == 0
    const = lambda bi, qi: (0, 0)
    return pl.pallas_call(
        _cross_kernel,
        grid=(b, nq),
        in_specs=[
            pl.BlockSpec((1, tq, d), lambda bi, qi: (bi, qi, 0)),
            pl.BlockSpec((1, d), const),
            pl.BlockSpec((d, dc), const),
            pl.BlockSpec((1, mlen, 2 * dc), lambda bi, qi: (bi, 0, 0)),
            pl.BlockSpec((1, DH_C), const),
            pl.BlockSpec((1, DH_C), const),
            pl.BlockSpec((dc, d), const),
            pl.BlockSpec((1, d), const),
            pl.BlockSpec((nr, d), const),
            pl.BlockSpec((nr, d), const),
            pl.BlockSpec((nr, 1), const),
        ],
        out_specs=[
            pl.BlockSpec((1, tq, d), lambda bi, qi: (bi, qi, 0)),
            pl.BlockSpec((tq * SUBLANES, LANES), lambda bi, qi: (bi * nq + qi, 0)),
            pl.BlockSpec((1, 8, tq), lambda bi, qi: (bi * nq + qi, 0, 0)),
        ],
        out_shape=[
            jax.ShapeDtypeStruct((b, s, d), F32),
            jax.ShapeDtypeStruct((b * s * SUBLANES, LANES), jnp.uint32),
            jax.ShapeDtypeStruct((b * nq, 8, tq), F32),
        ],
        input_output_aliases={0: 0},
        compiler_params=_cparams(("parallel", "parallel")),
        name="cross_block",
    )(x, g_cross.reshape(1, d), wq, kv, (g_qc * DH_C ** -0.5).reshape(1, DH_C), g_kc.reshape(1, DH_C),
      wo, g_ffn.reshape(1, d), wr_hi, wr_lo, b_r)


def _expert_kernel(idx_ref, be_ref, nr_ref, h_hbm, w1_ref, w3_ref, w2_ref, g_ref, z_hbm,
                   xbuf0, xbuf1, xbuf2, ybuf0, ybuf1, ybuf2, w1b, w3b, w2b, gsem, ssem, *, blk, n_steps):
    i = pl.program_id(0)
    n_cur = nr_ref[i]
    i_prev = jnp.maximum(i - 1, 0)
    n_prev = jnp.where(i >= 1, nr_ref[i_prev], 0)
    n_prev2 = jnp.where(i >= 2, nr_ref[jnp.maximum(i - 2, 0)], 0)
    slot = i % MOE_BUFS
    tok_mask = (1 << TOK_BITS) - 1
    xbuf = (xbuf0, xbuf1, xbuf2)
    ybuf = (ybuf0, ybuf1, ybuf2)
    rs = SUBLANES

    def tile_rows(row):
        return pl.ds(pl.multiple_of(row * rs, rs), rs)

    def gather_copy(step, r, sl):
        row = idx_ref[step * blk + r] & tok_mask
        return pltpu.make_async_copy(h_hbm.at[tile_rows(row)], xbuf[sl].at[tile_rows(r)], gsem.at[sl])

    def scatter_row(step, r):
        return lax.shift_right_logical(idx_ref[step * blk + r], TOK_BITS)

    def scatter_copy(row, r, sl):
        return pltpu.make_async_copy(ybuf[sl].at[tile_rows(r)], z_hbm.at[tile_rows(row)], ssem.at[sl])

    def issue_gather(step, sl):
        def body(r, carry):
            gather_copy(step, r, sl).start()
            return carry
        lax.fori_loop(0, blk, body, 0, unroll=8)

    def wait_gather(sl):
        pltpu.make_async_copy(h_hbm.at[pl.ds(0, blk * rs)], xbuf[sl], gsem.at[sl]).wait()

    def issue_scatter(step, sl, n):
        def body(r, carry):
            row = scatter_row(step, r)

            @pl.when(r < n)
            def _():
                scatter_copy(row, r, sl).start()
            return carry
        lax.fori_loop(0, blk, body, 0, unroll=8)

    def wait_scatter(sl, n):
        k = blk
        while k >= 1:
            @pl.when((n & k) != 0)
            def _(k=k):
                pltpu.make_async_copy(ybuf[sl].at[pl.ds(0, k * rs)], z_hbm.at[pl.ds(0, k * rs)],
                                      ssem.at[sl]).wait()
            k //= 2

    @pl.when(i == 0)
    def _():
        issue_gather(0, 0)
        issue_gather(1, 1)

    @pl.when((n_cur > 0) & ((i == 0) | (be_ref[i] != be_ref[jnp.maximum(i - 1, 0)])))
    def _():
        w1b[...] = w1_ref[0, 0].astype(BF16)
        w3b[...] = w3_ref[0, 0].astype(BF16)
        w2b[...] = w2_ref[0, 0].astype(BF16)

    def block_body(sl):
        nxt, prv = (sl + 1) % MOE_BUFS, (sl + 2) % MOE_BUFS
        wait_gather(sl)
        xl, xr = _unpack_bf16_pair(_load_token_major(xbuf[sl], blk))
        xl, xr = xl.astype(BF16), xr.astype(BF16)
        dh = xl.shape[1]
        for r in range(blk):
            gather_copy(i + 2, r, prv).start(priority=ROW_DMA_PRIORITY)
        for r in range(blk):
            row = scatter_row(i_prev, r)

            @pl.when(r < n_prev)
            def _(r=r, row=row):
                scatter_copy(row, r, prv).start(priority=ROW_DMA_PRIORITY)
        a1 = (jnp.dot(xl, w1b[0:dh, :], preferred_element_type=F32)
              + jnp.dot(xr, w1b[dh:, :], preferred_element_type=F32))
        a3 = (jnp.dot(xl, w3b[0:dh, :], preferred_element_type=F32)
              + jnp.dot(xr, w3b[dh:, :], preferred_element_type=F32))
        hid = (a1 * _sigmoid(a1) * a3 * g_ref[...]).astype(BF16)
        y = jnp.dot(hid, w2b[...], preferred_element_type=F32)
        wait_scatter(nxt, n_prev2)
        _store_token_major(ybuf[sl], _pack_bf16_pair(y))

        @pl.when(i == n_steps - 1)
        def _():
            wait_gather(nxt)
            wait_gather(prv)
            wait_scatter(prv, n_prev)
            issue_scatter(i, sl, n_cur)
            wait_scatter(sl, n_cur)

    for sl in range(MOE_BUFS):
        @pl.when((n_cur > 0) & (slot == sl))
        def _(sl=sl):
            block_body(sl)

    for sl in range(MOE_BUFS):
        @pl.when((n_cur == 0) & (n_prev > 0) & (slot == sl))
        def _(sl=sl):
            nxt, prv = (sl + 1) % MOE_BUFS, (sl + 2) % MOE_BUFS
            wait_gather(sl)
            wait_gather(nxt)
            wait_scatter(nxt, n_prev2)
            issue_scatter(i_prev, prv, n_prev)
            wait_scatter(prv, n_prev)


def _expert_mlp(hpack, idx, blk_expert, nreal, gate_buf, w1, w3, w2, layer, z_rows):
    d = w1.shape[2]
    assert d == 2 * SUBLANES * LANES and hpack.shape[1] == LANES
    n_steps = idx.shape[0] // MOE_BLK - (MOE_BUFS - 1)
    assert n_steps >= MOE_BUFS
    de = w1.shape[3]
    assert MOE_BLK & (MOE_BLK - 1) == 0
    return pl.pallas_call(
        functools.partial(_expert_kernel, blk=MOE_BLK, n_steps=n_steps),
        grid_spec=pltpu.PrefetchScalarGridSpec(
            num_scalar_prefetch=3,
            grid=(n_steps,),
            in_specs=[
                pl.BlockSpec(memory_space=pl.ANY),
                pl.BlockSpec((1, 1, d, de), lambda i, ix, be, nr: (layer, be[i], 0, 0)),
                pl.BlockSpec((1, 1, d, de), lambda i, ix, be, nr: (layer, be[i], 0, 0)),
                pl.BlockSpec((1, 1, de, d), lambda i, ix, be, nr: (layer, be[i], 0, 0)),
                pl.BlockSpec((MOE_BLK, 1), lambda i, ix, be, nr: (i, 0)),
            ],
            out_specs=pl.BlockSpec(memory_space=pl.ANY),
            scratch_shapes=(
                [pltpu.VMEM((MOE_BLK * SUBLANES, LANES), jnp.uint32)] * (2 * MOE_BUFS) + [
                    pltpu.VMEM((d, de), BF16),
                    pltpu.VMEM((d, de), BF16),
                    pltpu.VMEM((de, d), BF16),
                    pltpu.SemaphoreType.DMA((MOE_BUFS,)),
                    pltpu.SemaphoreType.DMA((MOE_BUFS,)),
                ]),
        ),
        out_shape=jax.ShapeDtypeStruct((z_rows * SUBLANES, LANES), jnp.uint32),
        compiler_params=_cparams(("arbitrary",)),
        name="expert_mlp",
    )(idx, blk_expert, nreal, hpack, w1, w3, w2, gate_buf)


def _combine_kernel(x_ref, *refs):
    z_refs, o_ref = refs[:-1], refs[-1]
    tm, d = x_ref.shape
    acc_hi = x_ref[:, 0:d // 2]
    acc_lo = x_ref[:, d // 2:]
    for z_ref in z_refs:
        hi, lo = _unpack_bf16_pair(_load_token_major(z_ref, tm))
        acc_hi = acc_hi + hi
        acc_lo = acc_lo + lo
    o_ref[:, 0:d // 2] = acc_hi
    o_ref[:, d // 2:] = acc_lo


def _combine(x, z):
    t, d = x.shape
    tm = min(COMB_TM, t)
    assert t % tm == 0
    nt = t // tm
    return pl.pallas_call(
        _combine_kernel,
        grid=(nt,),
        in_specs=[pl.BlockSpec((tm, d), lambda i: (i, 0))] + [
            pl.BlockSpec((tm * SUBLANES, LANES), lambda i, k=k: (k * nt + i, 0)) for k in range(TOP_K)],
        out_specs=pl.BlockSpec((tm, d), lambda i: (i, 0)),
        out_shape=jax.ShapeDtypeStruct((t, d), F32),
        input_output_aliases={0: 0},
        compiler_params=_cparams(("parallel",)),
        name="moe_combine",
    )(x, *([z] * TOP_K))


def _t5_causal_bucket(dist):
    max_exact = N_BUCKETS // 2
    d_f = jnp.maximum(dist, 1).astype(F32)
    large = max_exact + (jnp.log(d_f / max_exact) / math.log(MAX_DISTANCE / max_exact)
                         * (N_BUCKETS - max_exact)).astype(jnp.int32)
    large = jnp.minimum(large, N_BUCKETS - 1)
    return jnp.where(dist < max_exact, dist, large)


def _bias_tiles(rel_bias_table, s, t):
    assert t >= MAX_DISTANCE
    rbd = rel_bias_table[_t5_causal_bucket(jnp.arange(s))].T.astype(F32)
    far = rel_bias_table[N_BUCKETS - 1].astype(F32)
    rbd = rbd - far[:, None]
    h = rbd.shape[0]

    def toeplitz(w):
        a = jnp.broadcast_to(w[:, None, :], (h, t, 2 * t)).reshape(h, 2 * t * t)
        return a[:, :t * (2 * t - 1)].reshape(h, t, 2 * t - 1)[:, :, :t]

    k = np.arange(2 * t)
    d_diag = np.where(k == 0, 0, np.minimum(2 * t - k, s - 1))
    w_diag = jnp.where((k >= 1) & (k <= t), NEG_INF, rbd[:, d_diag])
    d_off = np.minimum(np.where(k < t, t - k, 3 * t - k), s - 1)
    w_off = rbd[:, d_off]
    off = toeplitz(w_off).transpose(0, 2, 1) * LOG2E
    diag = jnp.maximum(toeplitz(w_diag).transpose(0, 2, 1) * LOG2E, NEG_INF)
    return jnp.concatenate([off, off], axis=2), jnp.concatenate([diag, diag], axis=2)


def _moe_plan(rt, t, n_experts, blk):
    assert TOP_K == 2
    ids = rt[:, 0:TOP_K, :].astype(jnp.int32).transpose(0, 2, 1).reshape(t * TOP_K)
    gates = rt[:, TOP_K:2 * TOP_K, :].transpose(0, 2, 1).reshape(t * TOP_K)
    a = t * TOP_K
    _, order, sgate = lax.sort((ids, jnp.arange(a, dtype=jnp.int32), gates), num_keys=1)
    counts = jnp.sum((ids[None, :] == jnp.arange(n_experts)[:, None]).astype(jnp.int32), axis=1)
    start = jnp.cumsum(counts) - counts
    pcounts = (counts + blk - 1) // blk * blk
    pend = jnp.cumsum(pcounts)
    pstart = pend - pcounts
    n_blocks = -(-(a + n_experts * (blk - 1)) // blk)
    p = n_blocks * blk
    blk_row0 = jnp.arange(n_blocks, dtype=jnp.int32) * blk
    blk_expert = jnp.minimum(jnp.sum((pend[None, :] <= blk_row0[:, None]).astype(jnp.int32), axis=1),
                             n_experts - 1)
    nreal = jnp.clip((pstart + counts)[blk_expert] - blk_row0, 0, blk).astype(jnp.int32)
    win0 = (start + 0)[blk_expert] + blk_row0 - pstart[blk_expert]
    valid = (jnp.arange(blk, dtype=jnp.int32)[None, :] < nreal[:, None]).reshape(p)

    def windows(v):
        v = jnp.concatenate([v, jnp.zeros((blk,), v.dtype)])
        return jax.vmap(lambda w0: lax.dynamic_slice(v, (w0,), (blk,)))(jnp.clip(win0, 0, a)).reshape(p)

    asg = windows(order).astype(jnp.uint32)
    tok, kk = asg >> 1, asg & 1
    zrow = kk * t + tok
    idx = jnp.where(valid, (zrow << TOK_BITS) | tok, 0).astype(jnp.uint32)
    idx = lax.bitcast_convert_type(idx, jnp.int32)
    idx = jnp.concatenate([idx, jnp.zeros(((MOE_BUFS - 1) * blk,), jnp.int32)])
    gate_buf = jnp.where(valid, windows(sgate), 0.0)
    nreal = jnp.concatenate([nreal, jnp.zeros((1,), jnp.int32)])
    return idx, blk_expert.astype(jnp.int32), nreal, gate_buf.reshape(p, 1)


def kernel(x, mem, rel_bias_table, g_mix, w_in, g_q, g_k, diff_lambda, g_subln, conv_w, conv_b,
           conv_ln_g, conv_ln_b, w_out, g_cross, g_mem, wq_c, wkv_c, g_qc, g_kc, wo_c, g_ffn,
           w_group, b_group, w_router, b_router, w1, w3, w2):
    b, s, d = x.shape
    mlen = mem.shape[1]
    depth = w_in.shape[0]
    t = b * s
    d_conv = conv_w.shape[2]
    n_heads = (w_out.shape[1] - d_conv) // DV
    d_qk = n_heads * 2 * DQ
    n_experts = w_router.shape[2]
    assert t <= (1 << TOK_BITS) and TOP_K * t <= (1 << (32 - TOK_BITS))

    boff, bdiag = _bias_tiles(rel_bias_table, s, min(ATTN_T, s))
    bias_abs = jnp.maximum(jnp.max(jnp.abs(boff)),
                           jnp.max(jnp.where(bdiag > 0.5 * NEG_INF, jnp.abs(bdiag), 0.0)))
    n_router_rows = -(-(N_GROUPS + n_experts) // LANES) * LANES

    for l in range(depth):
        lam_init = 0.8 - 0.6 * math.exp(-0.3 * l)
        dl = diff_lambda[l].astype(F32)
        lam = jnp.exp(jnp.sum(dl[0] * dl[1])) - jnp.exp(jnp.sum(dl[2] * dl[3])) + lam_init
        gs = g_subln[l] * (1.0 - lam_init)
        qk_gain = jnp.concatenate([
            jnp.tile(g_q[l] * (DQ ** -0.5 * LOG2E), d_qk // DQ), jnp.tile(g_k[l], d_qk // DQ),
            jnp.ones((w_in.shape[2] - 2 * d_qk,), F32)]).reshape(1, -1)

        proj = _norm_matmul(x.reshape(t, d), g_mix[l], w_in[l].astype(BF16), qk_gain, 2 * d_qk)
        proj = proj.reshape(b, s, -1)
        score_bound = (jnp.max(jnp.abs(g_q[l])) * jnp.max(jnp.abs(g_k[l])) * (math.sqrt(DQ) * LOG2E * 1.02)
                       + bias_abs)
        attn = _diff_attention(proj, boff, bdiag, lam, gs, n_heads, score_bound)
        conv = _conformer_conv(proj, 2 * d_qk + n_heads * DV, conv_w[l], conv_b[l],
                               conv_ln_g[l], conv_ln_b[l])
        x = _outproj_residual(x.reshape(t, d), attn.reshape(t, -1), conv.reshape(t, -1),
                              w_out[l].astype(BF16), in_place=l > 0).reshape(b, s, d)

        kv = _norm_matmul(mem.reshape(b * mlen, d), g_mem[l], wkv_c[l].astype(BF16))
        wr = jnp.zeros((n_router_rows, d), F32)
        wr = wr.at[0:N_GROUPS].set(w_group[l].T).at[N_GROUPS:N_GROUPS + n_experts].set(w_router[l].T)
        wr_hi = wr.astype(BF16)
        wr_lo = (wr - wr_hi.astype(F32)).astype(BF16)
        b_r = jnp.zeros((n_router_rows, 1), F32)
        b_r = b_r.at[0:N_GROUPS, 0].set(b_group[l]).at[N_GROUPS:N_GROUPS + n_experts, 0].set(b_router[l])
        x, hpack, rt = _cross_block(x, kv.reshape(b, mlen, -1), g_cross[l], wq_c[l].astype(BF16),
                                    g_qc[l], g_kc[l], wo_c[l].astype(BF16), g_ffn[l], wr_hi, wr_lo, b_r)

        idx, blk_expert, nreal, gate_buf = _moe_plan(rt, t, n_experts, MOE_BLK)
        z = _expert_mlp(hpack, idx, blk_expert, nreal, gate_buf, w1, w3, w2, l, TOP_K * t)
        x = _combine(x.reshape(t, d), z).reshape(b, s, d)
    return x
```

```python
import functools
import math

import jax
import jax.numpy as jnp
import numpy as np
from jax import lax
from jax.experimental import pallas as pl
from jax.experimental.pallas import tpu as pltpu

DQ = 64
DV = 2 * DQ
CONV_WIDTH = 31
N_BUCKETS = 32
MAX_DISTANCE = 128
N_HEADS_C = 4
DH_C = 128
N_GROUPS = 8
EXPERTS_PER_GROUP = 8
TOP_K = 2
EPS = 1e-6
NEG_INF = -1e30
LOG2E = 1.4426950408889634
ATTN_SAFE_LOG2 = 96.0

LANES = 128
SUBLANES = 8
MXU_DIM = 256
VMEM_LIMIT_BYTES = 56 * 1024 * 1024

F32 = jnp.float32
BF16 = jnp.bfloat16

PROJ_TM = 1024
PROJ_TN = 512
ATTN_T = 256
CONV_TS = 256
CONV_HALO = 32
CONV_ROWS = 128
CROSS_TQ = 512
MOE_BLK = 256
COMB_TM = 256
TOK_BITS = 15
ROW_DMA_PRIORITY = 1
MOE_BUFS = 3


def _cparams(sem):
    return pltpu.CompilerParams(dimension_semantics=sem, vmem_limit_bytes=VMEM_LIMIT_BYTES)


def _sigmoid(x):
    return 1.0 / (1.0 + jnp.exp(-x))


def _store_token_major(ref, val):
    n = val.shape[0]
    for c in range(SUBLANES):
        ref[pl.ds(c, n, stride=SUBLANES), :] = val[:, c * LANES:(c + 1) * LANES]


def _load_token_major(ref, n):
    return jnp.concatenate([ref[pl.ds(c, n, stride=SUBLANES), :] for c in range(SUBLANES)], axis=1)


def _pack_bf16_pair(x):
    w = x.shape[1] // 2
    bits = lax.bitcast_convert_type(x.astype(BF16).astype(F32), jnp.uint32)
    return bits[:, 0:w] | (bits[:, w:] >> 16)


def _unpack_bf16_pair(word):
    hi = lax.bitcast_convert_type(word & jnp.uint32(0xFFFF0000), F32)
    lo = lax.bitcast_convert_type(word << 16, F32)
    return hi, lo


def _norm_matmul_kernel(x_ref, g_ref, w_ref, qkg_ref, seg_ref, o_ref, xn_ref, *, n_qk_tiles):
    j = pl.program_id(1)

    @pl.when(j == 0)
    def _():
        x = x_ref[...]
        inv = lax.rsqrt(jnp.mean(x * x, axis=-1, keepdims=True) + EPS)
        xn_ref[...] = (x * inv * g_ref[...]).astype(BF16)

    acc = jnp.dot(xn_ref[...], w_ref[...], preferred_element_type=F32)

    if n_qk_tiles == 0:
        o_ref[...] = acc.astype(o_ref.dtype)
        return

    @pl.when(j < n_qk_tiles)
    def _():
        sq = (acc * acc).astype(BF16)
        seg = seg_ref[...]
        parts = [jnp.dot(sq[:, c * MXU_DIM:(c + 1) * MXU_DIM], seg, preferred_element_type=F32)
                 for c in range(acc.shape[1] // MXU_DIM)]
        ss = jnp.concatenate(parts, axis=1)
        o_ref[...] = (acc * lax.rsqrt(ss * (1.0 / DQ) + EPS) * qkg_ref[...]).astype(o_ref.dtype)

    @pl.when(j >= n_qk_tiles)
    def _():
        o_ref[...] = acc.astype(o_ref.dtype)


def _norm_matmul(x, g, w, qk_gain=None, n_qk_cols=0):
    m, d = x.shape
    n = w.shape[1]
    tm = min(PROJ_TM, m)
    tn = min(PROJ_TN, n)
    assert m % tm == 0 and n % tn == 0 and n_qk_cols % tn == 0 and tn % MXU_DIM == 0
    if qk_gain is None:
        qk_gain = jnp.ones((1, n), F32)
    r = jnp.arange(MXU_DIM) // DQ
    seg = (r[:, None] == r[None, :]).astype(BF16)
    return pl.pallas_call(
        functools.partial(_norm_matmul_kernel, n_qk_tiles=n_qk_cols // tn),
        grid=(m // tm, n // tn),
        in_specs=[
            pl.BlockSpec((tm, d), lambda i, j: (i, 0)),
            pl.BlockSpec((1, d), lambda i, j: (0, 0)),
            pl.BlockSpec((d, tn), lambda i, j: (0, j)),
            pl.BlockSpec((1, tn), lambda i, j: (0, j)),
            pl.BlockSpec((MXU_DIM, MXU_DIM), lambda i, j: (0, 0)),
        ],
        out_specs=pl.BlockSpec((tm, tn), lambda i, j: (i, j)),
        out_shape=jax.ShapeDtypeStruct((m, n), BF16),
        scratch_shapes=[pltpu.VMEM((tm, d), BF16)],
        compiler_params=_cparams(("parallel", "arbitrary")),
        name="norm_matmul",
    )(x, g.reshape(1, d), w, qk_gain, seg)


def _attn_kernel(q_ref, k_ref, v_ref, boff_ref, bdiag_ref, lam_ref, gs_ref, o_ref,
                 qs_ref, vt_ref, m_ref, l_ref, acc_ref, *, t, bounded):
    i = pl.program_id(2)
    s_len = k_ref.shape[1]

    @pl.when(i == 0)
    def _():
        for c in range(s_len // t):
            vt_ref[c] = v_ref[0, c * t:(c + 1) * t, :].astype(F32).T.astype(BF16)

    q = q_ref[0].astype(F32)
    lane = lax.broadcasted_iota(jnp.int32, q.shape, 1)
    qs_ref[0:t, :] = jnp.where(lane < DQ, q, 0.0).astype(BF16)
    qs_ref[t:2 * t, :] = jnp.where(lane >= DQ, q, 0.0).astype(BF16)
    def scores(j):
        off = j * t if isinstance(j, int) else pl.multiple_of(j * t, t)
        k = k_ref[0, pl.ds(off, t), :]
        return lax.dot_general(k, qs_ref[...], (((1,), (1,)), ((), ())), preferred_element_type=F32)

    def finish(acc, l):
        ot = acc / l
        ot = ot[:, 0:t] - lam_ref[...] * ot[:, t:2 * t]
        inv = lax.rsqrt(jnp.mean(ot * ot, axis=0, keepdims=True) + EPS)
        o_ref[0] = (ot * inv * gs_ref[...]).T.astype(o_ref.dtype)

    if bounded:
        def sweep(ii):
            acc = l8 = None
            for j in range(ii + 1):
                st = scores(j)
                if j == ii:
                    st = st + bdiag_ref[0]
                elif j == ii - 1:
                    st = st + boff_ref[0]
                pt = jnp.exp2(st)
                lj = jnp.sum(pt.reshape(t // SUBLANES, SUBLANES, 2 * t), axis=0)
                pv = jnp.dot(vt_ref[j], pt.astype(BF16), preferred_element_type=F32)
                acc = pv if acc is None else acc + pv
                l8 = lj if l8 is None else l8 + lj
            finish(acc, jnp.sum(l8, axis=0, keepdims=True))

        for ii in range(s_len // t):
            pl.when(i == ii)(functools.partial(sweep, ii))
        return

    m_ref[...] = jnp.full(m_ref.shape, NEG_INF, F32)
    l_ref[...] = jnp.zeros(l_ref.shape, F32)
    acc_ref[...] = jnp.zeros(acc_ref.shape, F32)

    def softmax_pv(j, st):
        m_prev = m_ref[...]
        m_new = jnp.maximum(m_prev, jnp.max(st, axis=0, keepdims=True))
        alpha = jnp.exp2(m_prev - m_new)
        pt = jnp.exp2(st - m_new)
        l_ref[...] = alpha * l_ref[...] + jnp.sum(pt, axis=0, keepdims=True)
        pv = jnp.dot(vt_ref[j], pt.astype(BF16), preferred_element_type=F32)
        acc_ref[...] = alpha * acc_ref[...] + pv
        m_ref[...] = m_new

    def far_body(j, st):
        st_next = scores(j + 1)
        softmax_pv(j, st)
        return st_next

    def off_body(j, st):
        st_next = scores(j + 1)
        softmax_pv(j, st + boff_ref[0])
        return st_next

    st = lax.fori_loop(0, i - 1, far_body, scores(0))
    st = lax.fori_loop(jnp.maximum(i - 1, 0), i, off_body, st)
    softmax_pv(i, st + bdiag_ref[0])
    finish(acc_ref[...], l_ref[...])


def _diff_attention(proj, boff, bdiag, lam, gs, n_heads, score_bound):
    b, s, _ = proj.shape
    t = boff.shape[1]
    assert s % t == 0
    lam_row = jnp.full((1, t), lam, F32)
    gs_col = jnp.broadcast_to(gs.astype(F32)[:, None], (DV, t))
    operands = (proj, proj, proj, boff, bdiag, lam_row, gs_col)
    return lax.cond(score_bound <= ATTN_SAFE_LOG2,
                    functools.partial(_attn_call, t=t, n_heads=n_heads, bounded=True),
                    functools.partial(_attn_call, t=t, n_heads=n_heads, bounded=False),
                    *operands)


def _attn_call(*operands, t, n_heads, bounded):
    b, s, _ = operands[0].shape
    return pl.pallas_call(
        functools.partial(_attn_kernel, t=t, bounded=bounded),
        grid=(b, n_heads, s // t),
        in_specs=[
            pl.BlockSpec((1, t, DV), lambda bi, h, i: (bi, i, h)),
            pl.BlockSpec((1, s, DV), lambda bi, h, i: (bi, 0, n_heads + h)),
            pl.BlockSpec((1, s, DV), lambda bi, h, i: (bi, 0, 2 * n_heads + h)),
            pl.BlockSpec((1, t, 2 * t), lambda bi, h, i: (h, 0, 0)),
            pl.BlockSpec((1, t, 2 * t), lambda bi, h, i: (h, 0, 0)),
            pl.BlockSpec((1, t), lambda bi, h, i: (0, 0)),
            pl.BlockSpec((DV, t), lambda bi, h, i: (0, 0)),
        ],
        out_specs=pl.BlockSpec((1, t, DV), lambda bi, h, i: (bi, i, h)),
        out_shape=jax.ShapeDtypeStruct((b, s, n_heads * DV), BF16),
        scratch_shapes=[
            pltpu.VMEM((2 * t, DV), BF16),
            pltpu.VMEM((s // t, DV, t), BF16),
            pltpu.VMEM((1, 2 * t), F32),
            pltpu.VMEM((1, 2 * t), F32),
            pltpu.VMEM((DV, 2 * t), F32),
        ],
        compiler_params=_cparams(("parallel", "parallel", "arbitrary")),
        name="diff_attention_bounded" if bounded else "diff_attention",
    )(*operands)


def _conv_kernel(a_ref, gate_ref, w_ref, b_ref, lg_ref, lb_ref, o_ref, uext_ref, y_ref, win_ref, *, ts):
    si = pl.program_id(1)

    @pl.when(si == 0)
    def _():
        uext_ref[0:CONV_HALO, :] = jnp.zeros((CONV_HALO, uext_ref.shape[1]), F32)

    @pl.when(si > 0)
    def _():
        uext_ref[0:CONV_HALO, :] = uext_ref[ts:ts + CONV_HALO, :]

    a = a_ref[0].astype(F32)
    gate = gate_ref[0].astype(F32)
    uext_ref[CONV_HALO:CONV_HALO + ts, :] = a * _sigmoid(gate)

    base = CONV_HALO - (CONV_WIDTH - 1)
    rows = min(ts, CONV_ROWS)
    for c in range(uext_ref.shape[1] // LANES):
        cs = slice(c * LANES, (c + 1) * LANES)
        for r0 in range(0, ts, rows):
            acc = jnp.broadcast_to(b_ref[:, cs], (rows, LANES))
            for sh in range(SUBLANES):
                taps = [j for j in range(CONV_WIDTH) if (base + j) % SUBLANES == sh]
                span = max((base + j) // SUBLANES for j in taps) * SUBLANES + rows
                if sh:
                    win_ref[sh, 0:span, :] = uext_ref[r0 + sh:r0 + sh + span, cs]
                    win = win_ref[sh, 0:span, :]
                else:
                    win = uext_ref[r0:r0 + span, cs]
                for j in taps:
                    a0 = (base + j) // SUBLANES * SUBLANES
                    acc = acc + w_ref[j:j + 1, cs] * win[a0:a0 + rows]
            y_ref[r0:r0 + rows, cs] = acc

    y = y_ref[...]
    mu = jnp.mean(y, axis=-1, keepdims=True)
    yc = y - mu
    var = jnp.mean(yc * yc, axis=-1, keepdims=True)
    z = yc * lax.rsqrt(var + EPS) * lg_ref[...] + lb_ref[...]
    o_ref[0] = (z * _sigmoid(z)).astype(o_ref.dtype)


def _conformer_conv(proj, col0, conv_w, conv_b, ln_g, ln_b):
    b, s, _ = proj.shape
    c = conv_w.shape[1]
    ts = min(CONV_TS, s)
    assert s % ts == 0 and col0 % c == 0 and ts >= CONV_HALO
    return pl.pallas_call(
        functools.partial(_conv_kernel, ts=ts),
        grid=(b, s // ts),
        in_specs=[
            pl.BlockSpec((1, ts, c), lambda bi, si: (bi, si, col0 // c)),
            pl.BlockSpec((1, ts, c), lambda bi, si: (bi, si, col0 // c + 1)),
            pl.BlockSpec((CONV_WIDTH, c), lambda bi, si: (0, 0)),
            pl.BlockSpec((1, c), lambda bi, si: (0, 0)),
            pl.BlockSpec((1, c), lambda bi, si: (0, 0)),
            pl.BlockSpec((1, c), lambda bi, si: (0, 0)),
        ],
        out_specs=pl.BlockSpec((1, ts, c), lambda bi, si: (bi, si, 0)),
        out_shape=jax.ShapeDtypeStruct((b, s, c), BF16),
        scratch_shapes=[pltpu.VMEM((CONV_HALO + ts, c), F32), pltpu.VMEM((ts, c), F32),
                        pltpu.VMEM((SUBLANES, CONV_HALO + min(ts, CONV_ROWS), LANES), F32)],
        compiler_params=_cparams(("parallel", "arbitrary")),
        name="conformer_conv",
    )(proj, proj, conv_w, conv_b.reshape(1, c), ln_g.reshape(1, c), ln_b.reshape(1, c))


def _outproj_kernel(x_ref, a_ref, c_ref, w_ref, o_ref):
    ka = a_ref.shape[1]
    y = jnp.dot(a_ref[...], w_ref[0:ka, :], preferred_element_type=F32)
    y = y + jnp.dot(c_ref[...], w_ref[ka:, :], preferred_element_type=F32)
    o_ref[...] = x_ref[...] + y


def _outproj_residual(x, attn, conv, w, in_place):
    m, d = x.shape
    ka, kc = attn.shape[1], conv.shape[1]
    tm = min(PROJ_TM, m)
    tn = min(2 * PROJ_TN, d)
    assert m % tm == 0 and d % tn == 0
    return pl.pallas_call(
        _outproj_kernel,
        grid=(m // tm, d // tn),
        in_specs=[
            pl.BlockSpec((tm, tn), lambda i, j: (i, j)),
            pl.BlockSpec((tm, ka), lambda i, j: (i, 0)),
            pl.BlockSpec((tm, kc), lambda i, j: (i, 0)),
            pl.BlockSpec((ka + kc, tn), lambda i, j: (0, j)),
        ],
        out_specs=pl.BlockSpec((tm, tn), lambda i, j: (i, j)),
        out_shape=jax.ShapeDtypeStruct((m, d), F32),
        input_output_aliases={0: 0} if in_place else {},
        compiler_params=_cparams(("parallel", "parallel")),
        name="outproj_residual",
    )(x, attn, conv, w)


def _cross_kernel(x_ref, gc_ref, wq_ref, kv_ref, gq_ref, gk_ref, wo_ref, gf_ref,
                  wrh_ref, wrl_ref, br_ref, xo_ref, hp_ref, rt_ref):
    x = x_ref[0]
    d = x.shape[1]
    tq = x.shape[0]
    h = (x * lax.rsqrt(jnp.mean(x * x, axis=-1, keepdims=True) + EPS) * gc_ref[...]).astype(BF16)
    q = jnp.dot(h, wq_ref[...], preferred_element_type=F32)
    dc = N_HEADS_C * DH_C
    outs = []
    for hd in range(N_HEADS_C):
        cs = slice(hd * DH_C, (hd + 1) * DH_C)
        qh = q[:, cs]
        qh = qh * lax.rsqrt(jnp.mean(qh * qh, axis=-1, keepdims=True) + EPS) * gq_ref[...]
        kh = kv_ref[0, :, cs].astype(F32)
        kh = kh * lax.rsqrt(jnp.mean(kh * kh, axis=-1, keepdims=True) + EPS) * gk_ref[...]
        vh = kv_ref[0, :, dc + hd * DH_C:dc + (hd + 1) * DH_C]
        s = lax.dot_general(qh.astype(BF16), kh.astype(BF16), (((1,), (1,)), ((), ())),
                            preferred_element_type=F32)
        p = jnp.exp(s - jnp.max(s, axis=-1, keepdims=True))
        p = p / jnp.sum(p, axis=-1, keepdims=True)
        outs.append(jnp.dot(p.astype(BF16), vh, preferred_element_type=F32))
    o = jnp.concatenate(outs, axis=1).astype(BF16)
    x2 = x + jnp.dot(o, wo_ref[...], preferred_element_type=F32)
    xo_ref[0] = x2

    h2 = x2 * lax.rsqrt(jnp.mean(x2 * x2, axis=-1, keepdims=True) + EPS) * gf_ref[...]
    h2_hi = h2.astype(BF16)
    h2_hi32 = h2_hi.astype(F32)
    h2_lo = (h2 - h2_hi32).astype(BF16)

    _store_token_major(hp_ref, _pack_bf16_pair(h2_hi32))

    nt = (((1,), (1,)), ((), ()))
    lt = (lax.dot_general(wrh_ref[...], h2_hi, nt, preferred_element_type=F32)
          + lax.dot_general(wrh_ref[...], h2_lo, nt, preferred_element_type=F32)
          + lax.dot_general(wrl_ref[...], h2_hi, nt, preferred_element_type=F32)) + br_ref[...]
    g, e = N_GROUPS, EXPERTS_PER_GROUP
    gl = lt[0:g]
    rowg = lax.broadcasted_iota(jnp.int32, gl.shape, 0)
    gmax = jnp.max(gl, axis=0, keepdims=True)
    gsel = jnp.min(jnp.where(gl == gmax, rowg, g), axis=0, keepdims=True)
    gw = 1.0 / jnp.sum(jnp.exp(gl - gmax), axis=0, keepdims=True)
    el = jnp.zeros((e, tq), F32)
    for gi in range(g):
        el = jnp.where(gsel == gi, lt[g + gi * e:g + (gi + 1) * e], el)
    rowe = lax.broadcasted_iota(jnp.int32, el.shape, 0)
    v1 = jnp.max(el, axis=0, keepdims=True)
    i1 = jnp.min(jnp.where(el == v1, rowe, e), axis=0, keepdims=True)
    el2 = jnp.where(rowe == i1, -jnp.inf, el)
    v2 = jnp.max(el2, axis=0, keepdims=True)
    i2 = jnp.min(jnp.where(el2 == v2, rowe, e), axis=0, keepdims=True)
    e2 = jnp.exp(v2 - v1)
    den = 1.0 / (1.0 + e2)
    id1 = (gsel * e + i1).astype(F32)
    id2 = (gsel * e + i2).astype(F32)
    row8 = lax.broadcasted_iota(jnp.int32, (8, tq), 0)
    rt = jnp.where(row8 == 0, id1, jnp.where(row8 == 1, id2,
         jnp.where(row8 == 2, gw * den, jnp.where(row8 == 3, gw * e2 * den, 0.0))))
    rt_ref[0] = rt


def _cross_block(x, kv, g_cross, wq, g_qc, g_kc, wo, g_ffn, wr_hi, wr_lo, b_r):
    b, s, d = x.shape
    mlen = kv.shape[1]
    tq = min(CROSS_TQ, s)
    nq = s // tq
    dc = N_HEADS_C * DH_C
    nr = wr_hi.shape[0]
    assert s % tq == 0
    const = lambda bi, qi: (0, 0)
    return pl.pallas_call(
        _cross_kernel,
        grid=(b, nq),
        in_specs=[
            pl.BlockSpec((1, tq, d), lambda bi, qi: (bi, qi, 0)),
            pl.BlockSpec((1, d), const),
            pl.BlockSpec((d, dc), const),
            pl.BlockSpec((1, mlen, 2 * dc), lambda bi, qi: (bi, 0, 0)),
            pl.BlockSpec((1, DH_C), const),
            pl.BlockSpec((1, DH_C), const),
            pl.BlockSpec((dc, d), const),
            pl.BlockSpec((1, d), const),
            pl.BlockSpec((nr, d), const),
            pl.BlockSpec((nr, d), const),
            pl.BlockSpec((nr, 1), const),
        ],
        out_specs=[
            pl.BlockSpec((1, tq, d), lambda bi, qi: (bi, qi, 0)),
            pl.BlockSpec((tq * SUBLANES, LANES), lambda bi, qi: (bi * nq + qi, 0)),
            pl.BlockSpec((1, 8, tq), lambda bi, qi: (bi * nq + qi, 0, 0)),
        ],
        out_shape=[
            jax.ShapeDtypeStruct((b, s, d), F32),
            jax.ShapeDtypeStruct((b * s * SUBLANES, LANES), jnp.uint32),
            jax.ShapeDtypeStruct((b * nq, 8, tq), F32),
        ],
        input_output_aliases={0: 0},
        compiler_params=_cparams(("parallel", "parallel")),
        name="cross_block",
    )(x, g_cross.reshape(1, d), wq, kv, (g_qc * DH_C ** -0.5).reshape(1, DH_C), g_kc.reshape(1, DH_C),
      wo, g_ffn.reshape(1, d), wr_hi, wr_lo, b_r)


def _expert_kernel(idx_ref, be_ref, nr_ref, eo_ref, el_ref, nu_ref, h_hbm, w1_hbm, w3_hbm, w2_hbm, g_ref,
                   z_hbm, xbuf0, xbuf1, xbuf2, ybuf0, ybuf1, ybuf2, w1b, w3b, w2b, w1f, w3f, w2f,
                   gsem, ssem, wsem, *, blk, n_steps, layer):
    i = pl.program_id(0)
    n_cur = nr_ref[i]
    i_prev = jnp.maximum(i - 1, 0)
    n_prev = jnp.where(i >= 1, nr_ref[i_prev], 0)
    n_prev2 = jnp.where(i >= 2, nr_ref[jnp.maximum(i - 2, 0)], 0)
    slot = i % MOE_BUFS
    tok_mask = (1 << TOK_BITS) - 1
    xbuf = (xbuf0, xbuf1, xbuf2)
    ybuf = (ybuf0, ybuf1, ybuf2)
    rs = SUBLANES

    def tile_rows(row):
        return pl.ds(pl.multiple_of(row * rs, rs), rs)

    def gather_copy(step, r, sl):
        row = idx_ref[step * blk + r] & tok_mask
        return pltpu.make_async_copy(h_hbm.at[tile_rows(row)], xbuf[sl].at[tile_rows(r)], gsem.at[sl])

    def scatter_row(step, r):
        return lax.shift_right_logical(idx_ref[step * blk + r], TOK_BITS)

    def scatter_copy(row, r, sl):
        return pltpu.make_async_copy(ybuf[sl].at[tile_rows(r)], z_hbm.at[tile_rows(row)], ssem.at[sl])

    def issue_gather(step, sl):
        def body(r, carry):
            gather_copy(step, r, sl).start()
            return carry
        lax.fori_loop(0, blk, body, 0, unroll=8)

    def wait_gather(sl):
        pltpu.make_async_copy(h_hbm.at[pl.ds(0, blk * rs)], xbuf[sl], gsem.at[sl]).wait()

    def issue_scatter(step, sl, n):
        def body(r, carry):
            row = scatter_row(step, r)

            @pl.when(r < n)
            def _():
                scatter_copy(row, r, sl).start()
            return carry
        lax.fori_loop(0, blk, body, 0, unroll=8)

    def wait_scatter(sl, n):
        k = blk
        while k >= 1:
            @pl.when((n & k) != 0)
            def _(k=k):
                pltpu.make_async_copy(ybuf[sl].at[pl.ds(0, k * rs)], z_hbm.at[pl.ds(0, k * rs)],
                                      ssem.at[sl]).wait()
            k //= 2

    @pl.when(i == 0)
    def _():
        issue_gather(0, 0)
        issue_gather(1, 1)

    def weight_copies(ordinal, ws):
        e = el_ref[ordinal]
        return [pltpu.make_async_copy(src.at[layer, e], dst.at[ws], wsem.at[n, ws])
                for n, (src, dst) in enumerate(((w1_hbm, w1f), (w3_hbm, w3f), (w2_hbm, w2f)))]

    @pl.when(i == 0)
    def _():
        for c in weight_copies(0, 0):
            c.start()

    @pl.when((n_cur > 0) & ((i == 0) | (be_ref[i] != be_ref[jnp.maximum(i - 1, 0)])))
    def _():
        k = eo_ref[i]
        ws = k % 2

        @pl.when(k + 1 < nu_ref[0])
        def _():
            for c in weight_copies(k + 1, 1 - ws):
                c.start()
        for c in weight_copies(k, ws):
            c.wait()
        w1b[...] = w1f[ws].astype(BF16)
        w3b[...] = w3f[ws].astype(BF16)
        w2b[...] = w2f[ws].astype(BF16)

    def block_body(sl):
        nxt, prv = (sl + 1) % MOE_BUFS, (sl + 2) % MOE_BUFS
        wait_gather(sl)
        xl, xr = _unpack_bf16_pair(_load_token_major(xbuf[sl], blk))
        xl, xr = xl.astype(BF16), xr.astype(BF16)
        dh = xl.shape[1]
        for r in range(blk):
            gather_copy(i + 2, r, prv).start(priority=ROW_DMA_PRIORITY)
        for r in range(blk):
            row = scatter_row(i_prev, r)

            @pl.when(r < n_prev)
            def _(r=r, row=row):
                scatter_copy(row, r, prv).start(priority=ROW_DMA_PRIORITY)
        a1 = (jnp.dot(xl, w1b[0:dh, :], preferred_element_type=F32)
              + jnp.dot(xr, w1b[dh:, :], preferred_element_type=F32))
        a3 = (jnp.dot(xl, w3b[0:dh, :], preferred_element_type=F32)
              + jnp.dot(xr, w3b[dh:, :], preferred_element_type=F32))
        hid = (a1 * _sigmoid(a1) * a3 * g_ref[...]).astype(BF16)
        y = jnp.dot(hid, w2b[...], preferred_element_type=F32)
        wait_scatter(nxt, n_prev2)
        _store_token_major(ybuf[sl], _pack_bf16_pair(y))

        @pl.when(i == n_steps - 1)
        def _():
            wait_gather(nxt)
            wait_gather(prv)
            wait_scatter(prv, n_prev)
            issue_scatter(i, sl, n_cur)
            wait_scatter(sl, n_cur)

    for sl in range(MOE_BUFS):
        @pl.when((n_cur > 0) & (slot == sl))
        def _(sl=sl):
            block_body(sl)

    for sl in range(MOE_BUFS):
        @pl.when((n_cur == 0) & (n_prev > 0) & (slot == sl))
        def _(sl=sl):
            nxt, prv = (sl + 1) % MOE_BUFS, (sl + 2) % MOE_BUFS
            wait_gather(sl)
            wait_gather(nxt)
            wait_scatter(nxt, n_prev2)
            issue_scatter(i_prev, prv, n_prev)
            wait_scatter(prv, n_prev)


def _expert_mlp(hpack, idx, blk_expert, nreal, ring, gate_buf, w1, w3, w2, layer, z_rows):
    d = w1.shape[2]
    assert d == 2 * SUBLANES * LANES and hpack.shape[1] == LANES
    n_steps = idx.shape[0] // MOE_BLK - (MOE_BUFS - 1)
    assert n_steps >= MOE_BUFS
    de = w1.shape[3]
    assert MOE_BLK & (MOE_BLK - 1) == 0
    return pl.pallas_call(
        functools.partial(_expert_kernel, blk=MOE_BLK, n_steps=n_steps, layer=layer),
        grid_spec=pltpu.PrefetchScalarGridSpec(
            num_scalar_prefetch=6,
            grid=(n_steps,),
            in_specs=[
                pl.BlockSpec(memory_space=pl.ANY),
                pl.BlockSpec(memory_space=pl.ANY),
                pl.BlockSpec(memory_space=pl.ANY),
                pl.BlockSpec(memory_space=pl.ANY),
                pl.BlockSpec((MOE_BLK, 1), lambda i, *_: (i, 0)),
            ],
            out_specs=pl.BlockSpec(memory_space=pl.ANY),
            scratch_shapes=(
                [pltpu.VMEM((MOE_BLK * SUBLANES, LANES), jnp.uint32)] * (2 * MOE_BUFS) + [
                    pltpu.VMEM((d, de), BF16),
                    pltpu.VMEM((d, de), BF16),
                    pltpu.VMEM((de, d), BF16),
                    pltpu.VMEM((2, d, de), F32),
                    pltpu.VMEM((2, d, de), F32),
                    pltpu.VMEM((2, de, d), F32),
                    pltpu.SemaphoreType.DMA((MOE_BUFS,)),
                    pltpu.SemaphoreType.DMA((MOE_BUFS,)),
                    pltpu.SemaphoreType.DMA((3, 2)),
                ]),
        ),
        out_shape=jax.ShapeDtypeStruct((z_rows * SUBLANES, LANES), jnp.uint32),
        compiler_params=_cparams(("arbitrary",)),
        name="expert_mlp",
    )(idx, blk_expert, nreal, *ring, hpack, w1, w3, w2, gate_buf)


def _combine_kernel(x_ref, *refs):
    z_refs, o_ref = refs[:-1], refs[-1]
    tm, d = x_ref.shape
    acc_hi = x_ref[:, 0:d // 2]
    acc_lo = x_ref[:, d // 2:]
    for z_ref in z_refs:
        hi, lo = _unpack_bf16_pair(_load_token_major(z_ref, tm))
        acc_hi = acc_hi + hi
        acc_lo = acc_lo + lo
    o_ref[:, 0:d // 2] = acc_hi
    o_ref[:, d // 2:] = acc_lo


def _combine(x, z):
    t, d = x.shape
    tm = min(COMB_TM, t)
    assert t % tm == 0
    nt = t // tm
    return pl.pallas_call(
        _combine_kernel,
        grid=(nt,),
        in_specs=[pl.BlockSpec((tm, d), lambda i: (i, 0))] + [
            pl.BlockSpec((tm * SUBLANES, LANES), lambda i, k=k: (k * nt + i, 0)) for k in range(TOP_K)],
        out_specs=pl.BlockSpec((tm, d), lambda i: (i, 0)),
        out_shape=jax.ShapeDtypeStruct((t, d), F32),
        input_output_aliases={0: 0},
        compiler_params=_cparams(("parallel",)),
        name="moe_combine",
    )(x, *([z] * TOP_K))


def _t5_causal_bucket(dist):
    max_exact = N_BUCKETS // 2
    d_f = jnp.maximum(dist, 1).astype(F32)
    large = max_exact + (jnp.log(d_f / max_exact) / math.log(MAX_DISTANCE / max_exact)
                         * (N_BUCKETS - max_exact)).astype(jnp.int32)
    large = jnp.minimum(large, N_BUCKETS - 1)
    return jnp.where(dist < max_exact, dist, large)


def _bias_tiles(rel_bias_table, s, t):
    assert t >= MAX_DISTANCE
    rbd = rel_bias_table[_t5_causal_bucket(jnp.arange(s))].T.astype(F32)
    far = rel_bias_table[N_BUCKETS - 1].astype(F32)
    rbd = rbd - far[:, None]
    h = rbd.shape[0]

    def toeplitz(w):
        a = jnp.broadcast_to(w[:, None, :], (h, t, 2 * t)).reshape(h, 2 * t * t)
        return a[:, :t * (2 * t - 1)].reshape(h, t, 2 * t - 1)[:, :, :t]

    k = np.arange(2 * t)
    d_diag = np.where(k == 0, 0, np.minimum(2 * t - k, s - 1))
    w_diag = jnp.where((k >= 1) & (k <= t), NEG_INF, rbd[:, d_diag])
    d_off = np.minimum(np.where(k < t, t - k, 3 * t - k), s - 1)
    w_off = rbd[:, d_off]
    off = toeplitz(w_off).transpose(0, 2, 1) * LOG2E
    diag = jnp.maximum(toeplitz(w_diag).transpose(0, 2, 1) * LOG2E, NEG_INF)
    return jnp.concatenate([off, off], axis=2), jnp.concatenate([diag, diag], axis=2)


def _moe_plan(rt, t, n_experts, blk):
    nblk_rt, _, tq = rt.shape
    ids = rt[:, 0:TOP_K, :].astype(jnp.int32).transpose(0, 2, 1).reshape(t * TOP_K)
    gates = rt[:, TOP_K:2 * TOP_K, :].transpose(0, 2, 1).reshape(t * TOP_K)
    a = t * TOP_K
    order = jnp.argsort(ids).astype(jnp.int32)
    counts = jnp.sum((ids[None, :] == jnp.arange(n_experts)[:, None]).astype(jnp.int32), axis=1)
    start = jnp.cumsum(counts) - counts
    pcounts = (counts + blk - 1) // blk * blk
    pend = jnp.cumsum(pcounts)
    pstart = pend - pcounts
    n_blocks = -(-(a + n_experts * (blk - 1)) // blk)
    p = n_blocks * blk
    blk_row0 = jnp.arange(n_blocks, dtype=jnp.int32) * blk
    blk_expert = jnp.minimum(jnp.sum((pend[None, :] <= blk_row0[:, None]).astype(jnp.int32), axis=1),
                             n_experts - 1)
    nreal = jnp.clip((pstart + counts)[blk_expert] - blk_row0, 0, blk).astype(jnp.int32)
    within = (blk_row0 - pstart[blk_expert])[:, None] + jnp.arange(blk, dtype=jnp.int32)[None, :]
    valid = jnp.arange(blk, dtype=jnp.int32)[None, :] < nreal[:, None]
    src = jnp.where(valid, start[blk_expert][:, None] + within, 0).reshape(p)
    valid = valid.reshape(p)
    asg = order[src].astype(jnp.uint32)
    zrow = (asg % TOP_K) * t + asg // TOP_K
    idx = jnp.where(valid, (zrow << TOK_BITS) | (asg // TOP_K), 0).astype(jnp.uint32)
    idx = lax.bitcast_convert_type(idx, jnp.int32)
    idx = jnp.concatenate([idx, jnp.zeros(((MOE_BUFS - 1) * blk,), jnp.int32)])
    gate_buf = jnp.where(valid, gates[order[src]], 0.0)
    nreal = jnp.concatenate([nreal, jnp.zeros((1,), jnp.int32)])
    used = counts > 0
    eids = jnp.arange(n_experts, dtype=jnp.int32)
    ring = ((jnp.cumsum(used.astype(jnp.int32)) - 1)[blk_expert].astype(jnp.int32),
            jnp.sort(jnp.where(used, eids, eids + n_experts)) % n_experts,
            jnp.sum(used.astype(jnp.int32)).reshape(1))
    return idx, blk_expert.astype(jnp.int32), nreal, ring, gate_buf.reshape(p, 1)


def kernel(x, mem, rel_bias_table, g_mix, w_in, g_q, g_k, diff_lambda, g_subln, conv_w, conv_b,
           conv_ln_g, conv_ln_b, w_out, g_cross, g_mem, wq_c, wkv_c, g_qc, g_kc, wo_c, g_ffn,
           w_group, b_group, w_router, b_router, w1, w3, w2):
    b, s, d = x.shape
    mlen = mem.shape[1]
    depth = w_in.shape[0]
    t = b * s
    d_conv = conv_w.shape[2]
    n_heads = (w_out.shape[1] - d_conv) // DV
    d_qk = n_heads * 2 * DQ
    n_experts = w_router.shape[2]
    assert t <= (1 << TOK_BITS) and TOP_K * t <= (1 << (32 - TOK_BITS))

    boff, bdiag = _bias_tiles(rel_bias_table, s, min(ATTN_T, s))
    bias_abs = jnp.maximum(jnp.max(jnp.abs(boff)),
                           jnp.max(jnp.where(bdiag > 0.5 * NEG_INF, jnp.abs(bdiag), 0.0)))
    n_router_rows = -(-(N_GROUPS + n_experts) // LANES) * LANES

    for l in range(depth):
        lam_init = 0.8 - 0.6 * math.exp(-0.3 * l)
        dl = diff_lambda[l].astype(F32)
        lam = jnp.exp(jnp.sum(dl[0] * dl[1])) - jnp.exp(jnp.sum(dl[2] * dl[3])) + lam_init
        gs = g_subln[l] * (1.0 - lam_init)
        qk_gain = jnp.concatenate([
            jnp.tile(g_q[l] * (DQ ** -0.5 * LOG2E), d_qk // DQ), jnp.tile(g_k[l], d_qk // DQ),
            jnp.ones((w_in.shape[2] - 2 * d_qk,), F32)]).reshape(1, -1)

        proj = _norm_matmul(x.reshape(t, d), g_mix[l], w_in[l].astype(BF16), qk_gain, 2 * d_qk)
        proj = proj.reshape(b, s, -1)
        score_bound = (jnp.max(jnp.abs(g_q[l])) * jnp.max(jnp.abs(g_k[l])) * (math.sqrt(DQ) * LOG2E * 1.02)
                       + bias_abs)
        attn = _diff_attention(proj, boff, bdiag, lam, gs, n_heads, score_bound)
        conv = _conformer_conv(proj, 2 * d_qk + n_heads * DV, conv_w[l], conv_b[l],
                               conv_ln_g[l], conv_ln_b[l])
        x = _outproj_residual(x.reshape(t, d), attn.reshape(t, -1), conv.reshape(t, -1),
                              w_out[l].astype(BF16), in_place=l > 0).reshape(b, s, d)

        kv = _norm_matmul(mem.reshape(b * mlen, d), g_mem[l], wkv_c[l].astype(BF16))
        wr = jnp.zeros((n_router_rows, d), F32)
        wr = wr.at[0:N_GROUPS].set(w_group[l].T).at[N_GROUPS:N_GROUPS + n_experts].set(w_router[l].T)
        wr_hi = wr.astype(BF16)
        wr_lo = (wr - wr_hi.astype(F32)).astype(BF16)
        b_r = jnp.zeros((n_router_rows, 1), F32)
        b_r = b_r.at[0:N_GROUPS, 0].set(b_group[l]).at[N_GROUPS:N_GROUPS + n_experts, 0].set(b_router[l])
        x, hpack, rt = _cross_block(x, kv.reshape(b, mlen, -1), g_cross[l], wq_c[l].astype(BF16),
                                    g_qc[l], g_kc[l], wo_c[l].astype(BF16), g_ffn[l], wr_hi, wr_lo, b_r)

        idx, blk_expert, nreal, ring, gate_buf = _moe_plan(rt, t, n_experts, MOE_BLK)
        z = _expert_mlp(hpack, idx, blk_expert, nreal, ring, gate_buf, w1, w3, w2, l, TOP_K * t)
        x = _combine(x.reshape(t, d), z).reshape(b, s, d)
    return x
```
